```python
import math
import jax, jax.numpy as jnp
from jax import lax
import numpy as np

D_MODEL = 4096
BATCH = 2
SEQ = 4096
DEPTH = 2
DEC_BATCH = 2
DEC_SEQ = 8192
PAST_LEN = 128

N_META = 16
MIX_WIDTH = D_MODEL
N_HEADS = D_MODEL // 256
NOPE_DIM = 128
ROPE_DIM = 64
QK_DIM = NOPE_DIM + ROPE_DIM
V_DIM = 128
ATTN_WIDTH = N_HEADS * V_DIM
Q_LORA = D_MODEL // 4
KV_LORA = D_MODEL // 8
ROPE_THETA = 10000.0
Q_BLOCK = 128
HYENA_WIDTH = MIX_WIDTH - ATTN_WIDTH
OUT_GROUP = 128
HYENA_ORDER = 2
FILTER_EMB = 33
FILTER_HIDDEN = 64
FAST_DECAY_PCT = 0.3
SLOW_DECAY_PCT = 1.5
DECAY_TARGET = 1e-2
IN_COLS = Q_LORA + KV_LORA + ROPE_DIM + (HYENA_ORDER + 1) * HYENA_WIDTH
N_GROUPS = 8
EXPERTS_PER_GROUP = 8
N_EXPERTS = N_GROUPS * EXPERTS_PER_GROUP
TOP_K_EXPERT = 2
D_EXPERT = D_MODEL // 4
MOE_BLOCK = 256
EPS = 1e-6

kernel_name = "hymba_mla_hyena_hmoe_encoder"


def rms_norm(x, g):
    xf = x.astype(jnp.float32)
    y = xf * lax.rsqrt(jnp.mean(xf * xf, axis=-1, keepdims=True) + EPS)
    return (y * g.astype(jnp.float32)).astype(x.dtype)


def rope_tail(x, cos, sin):
    nope, pe = x[..., :NOPE_DIM], x[..., NOPE_DIM:]
    half = ROPE_DIM // 2
    p1, p2 = pe[..., :half], pe[..., half:]
    c = cos[None, :, None, :].astype(x.dtype)
    s = sin[None, :, None, :].astype(x.dtype)
    return jnp.concatenate([nope, p1 * c - p2 * s, p2 * c + p1 * s], axis=-1)


def dense_block_attention(q, k, v):
    B, T, H, _ = q.shape
    nb = -(-T // Q_BLOCK)
    qp = jnp.pad(q, ((0, 0), (0, nb * Q_BLOCK - T), (0, 0), (0, 0)))
    qb = qp.reshape(B, nb, Q_BLOCK, H, QK_DIM).transpose(1, 0, 2, 3, 4)
    scale = QK_DIM ** -0.5

    def one_block(qblk):
        s = jnp.einsum('bqhd,bkhd->bhqk', qblk, k).astype(jnp.float32) * scale
        p = jax.nn.softmax(s, axis=-1).astype(v.dtype)
        return jnp.einsum('bhqk,bkhd->bqhd', p, v)

    o = lax.map(one_block, qb)
    return o.transpose(1, 0, 2, 3, 4).reshape(B, nb * Q_BLOCK, H, V_DIM)[:, :T]


def mla_group(h_cq, h_ckv, h_kpe, q_latent_g, kv_latent_g, w_uq, w_uk, w_uv, q_norm_g, k_norm_g):
    B, T, _ = h_cq.shape
    c_q = rms_norm(h_cq, q_latent_g)
    c_kv = rms_norm(h_ckv, kv_latent_g)
    q = (c_q @ w_uq).reshape(B, T, N_HEADS, QK_DIM)
    k_nope = (c_kv @ w_uk).reshape(B, T, N_HEADS, NOPE_DIM)
    v = (c_kv @ w_uv).reshape(B, T, N_HEADS, V_DIM)
    k_pe = jnp.broadcast_to(h_kpe[:, :, None, :], (B, T, N_HEADS, ROPE_DIM))
    k = jnp.concatenate([k_nope, k_pe], axis=-1)
    q = rms_norm(q, q_norm_g)
    k = rms_norm(k, k_norm_g)
    half = ROPE_DIM // 2
    pos = jnp.arange(T, dtype=jnp.float32)
    inv = ROPE_THETA ** (-jnp.arange(half, dtype=jnp.float32) / half)
    ang = pos[:, None] * inv[None, :]
    cos, sin = jnp.cos(ang), jnp.sin(ang)
    q = rope_tail(q, cos, sin)
    k = rope_tail(k, cos, sin)
    o = dense_block_attention(q, k, v)
    return o.reshape(B, T, ATTN_WIDTH)


def hyena_filter_spectra(length, w1, b1, f1, w2, b2, f2, w3):
    f32 = jnp.float32
    bands = (FILTER_EMB - 1) // 2
    t = jnp.linspace(0.0, 1.0, length, dtype=f32)[:, None]
    ang = 2.0 * math.pi * jnp.arange(length, dtype=f32)[:, None] / length
    freqs = jnp.linspace(1e-4, bands - 1, bands, dtype=f32)[None, :]
    feats = jnp.concatenate([t, jnp.cos(freqs * ang), -jnp.sin(freqs * ang)], axis=-1)
    hid = jnp.sin(f1.astype(f32) * (feats @ w1.astype(f32) + b1.astype(f32)))
    hid = jnp.sin(f2.astype(f32) * (hid @ w2.astype(f32) + b2.astype(f32)))
    filt = (hid @ w3.astype(f32)).reshape(length, HYENA_ORDER, 2, HYENA_WIDTH)
    deltas = jnp.abs(jnp.linspace(math.log(DECAY_TARGET) / SLOW_DECAY_PCT,
                                  math.log(DECAY_TARGET) / FAST_DECAY_PCT, HYENA_WIDTH, dtype=f32))
    filt = filt * jnp.exp(-t * deltas[None, :])[:, None, None, :]
    taps = jnp.concatenate([filt[:, :, 0],
                            jnp.zeros((1, HYENA_ORDER, HYENA_WIDTH), f32),
                            filt[:0:-1, :, 1]], axis=0)
    return jnp.fft.rfft(taps, axis=0)


def bidir_long_conv(z, spec):
    T = z.shape[1]
    zf = jnp.fft.rfft(z.astype(jnp.float32), n=2 * T, axis=1)
    y = jnp.fft.irfft(zf * spec[None], n=2 * T, axis=1)[:, :T]
    return y.astype(z.dtype)


def hyena_group(u, conv_w, conv_b, w1, b1, f1, w2, b2, f2, w3, hyena_d):
    B, T, _ = u.shape
    up = jnp.pad(u, ((0, 0), (1, 1), (0, 0)))
    uc = up[:, :-2] * conv_w[0] + up[:, 1:-1] * conv_w[1] + up[:, 2:] * conv_w[2] + conv_b
    v, x1, x2 = jnp.split(uc, 3, axis=-1)
    spec = hyena_filter_spectra(T, w1, b1, f1, w2, b2, f2, w3)
    z = v
    for n, gate in enumerate((x1, x2)):
        z = gate * (bidir_long_conv(z, spec[:, n]) + hyena_d[n] * z)
    return z


def moe_dispatch(xf, expert_ids, expert_w, w_gate, w_up, w_down):
    n_tok, d = xf.shape
    n_slots = n_tok * TOP_K_EXPERT
    flat_e = expert_ids.reshape(-1).astype(jnp.int32)
    flat_tok = jnp.arange(n_slots, dtype=jnp.int32) // TOP_K_EXPERT
    flat_w = expert_w.reshape(-1).astype(xf.dtype)
    order = jnp.argsort(flat_e).astype(jnp.int32)
    sorted_e = flat_e[order]
    counts = jnp.bincount(flat_e, length=N_EXPERTS).astype(jnp.int32)
    padded = (counts + MOE_BLOCK - 1) // MOE_BLOCK * MOE_BLOCK
    start = jnp.cumsum(counts) - counts
    pend = jnp.cumsum(padded)
    pstart = pend - padded
    dest = pstart[sorted_e] + (jnp.arange(n_slots, dtype=jnp.int32) - start[sorted_e])
    n_blocks = -(-n_slots // MOE_BLOCK) + N_EXPERTS
    cap = n_blocks * MOE_BLOCK
    slot_at = jnp.full((cap,), n_slots, jnp.int32).at[dest].set(order)
    tok_at = jnp.concatenate([flat_tok, jnp.full((1,), n_tok, jnp.int32)])[slot_at]
    w_at = jnp.concatenate([flat_w, jnp.zeros((1,), xf.dtype)])[slot_at]
    blk_expert = jnp.minimum(
        jnp.searchsorted(pend, jnp.arange(n_blocks, dtype=jnp.int32) * MOE_BLOCK, side='right'),
        N_EXPERTS - 1).astype(jnp.int32)
    x_ext = jnp.concatenate([xf, jnp.zeros((1, d), xf.dtype)], axis=0)

    def run_block(args):
        toks, e = args
        xb = x_ext[toks]
        hb = jax.nn.silu(xb @ w_gate[e]) * (xb @ w_up[e])
        return hb @ w_down[e]

    yb = lax.map(run_block, (tok_at.reshape(n_blocks, MOE_BLOCK), blk_expert))
    y = jax.ops.segment_sum(yb.reshape(cap, d) * w_at[:, None], tok_at, num_segments=n_tok + 1)
    return y[:n_tok]


def hier_moe(h, w_rg, b_rg, w_re, b_re, w_gate, w_up, w_down):
    B, T, D = h.shape
    n = B * T
    hf = h.reshape(n, D)
    g_prob = jax.nn.softmax((hf @ w_rg).astype(jnp.float32) + b_rg.astype(jnp.float32), axis=-1)
    g_top, g_idx = lax.top_k(g_prob, 1)
    e_logits = ((hf @ w_re).astype(jnp.float32) + b_re.astype(jnp.float32)).reshape(n, N_GROUPS, EXPERTS_PER_GROUP)
    in_group = jnp.take_along_axis(e_logits, g_idx[:, :, None], axis=1)[:, 0]
    e_top, e_idx = lax.top_k(in_group, TOP_K_EXPERT)
    weights = g_top * jax.nn.softmax(e_top, axis=-1)
    expert_ids = g_idx * EXPERTS_PER_GROUP + e_idx
    return moe_dispatch(hf, expert_ids, weights, w_gate, w_up, w_down).reshape(B, T, D)


def trunk(x, p):
    B = x.shape[0]
    meta = jnp.broadcast_to(p['meta_tokens'][None].astype(x.dtype), (B, N_META, D_MODEL))
    x = jnp.concatenate([meta, x], axis=1)
    T = x.shape[1]
    o1, o2, o3 = Q_LORA, Q_LORA + KV_LORA, Q_LORA + KV_LORA + ROPE_DIM
    for l in range(DEPTH):
        h = rms_norm(x, p['norm_mix_g'][l])
        proj = h @ p['w_in'][l]
        a = mla_group(proj[..., :o1], proj[..., o1:o2], proj[..., o2:o3],
                      p['q_latent_g'][l], p['kv_latent_g'][l], p['w_uq'][l], p['w_uk'][l], p['w_uv'][l],
                      p['q_norm_g'][l], p['k_norm_g'][l])
        hy = hyena_group(proj[..., o3:], p['conv_w'][l], p['conv_b'][l],
                         p['filt_w1'][l], p['filt_b1'][l], p['filt_freq1'][l],
                         p['filt_w2'][l], p['filt_b2'][l], p['filt_freq2'][l],
                         p['filt_w3'][l], p['hyena_d'][l])
        mixed = jnp.concatenate([a, hy], axis=-1).reshape(B, T, MIX_WIDTH // OUT_GROUP, OUT_GROUP)
        mixed = rms_norm(mixed, jnp.ones((OUT_GROUP,), jnp.float32)).reshape(B, T, MIX_WIDTH)
        mixed = mixed * p['out_norm_g'][l]
        x = x + mixed @ p['w_out'][l]
        h = rms_norm(x, p['norm_ffn_g'][l])
        x = x + hier_moe(h, p['w_router_group'][l], p['b_router_group'][l],
                         p['w_router_expert'][l], p['b_router_expert'][l],
                         p['w_gate'][l], p['w_up'][l], p['w_down'][l])
    return x[:, N_META:]


def setup_inputs(seed: int = 0) -> dict:
    key = jax.random.key(seed)
    ks = jax.random.split(key, 32)
    f32 = jnp.float32

    def nrm(k, shape, scale):
        return jax.random.normal(k, shape, f32) * scale

    def gain(k, shape):
        return 1.0 + 0.01 * jax.random.normal(k, shape, f32)

    L = DEPTH
    return {
        "x_prompt": nrm(ks[0], (BATCH, SEQ, D_MODEL), 1.0),
        "x_sample": nrm(ks[1], (DEC_BATCH, DEC_SEQ, D_MODEL), 1.0),
        "meta_tokens": nrm(ks[2], (N_META, D_MODEL), 1.0),
        "norm_mix_g": gain(ks[3], (L, D_MODEL)),
        "w_in": nrm(ks[4], (L, D_MODEL, IN_COLS), D_MODEL ** -0.5),
        "q_latent_g": gain(ks[5], (L, Q_LORA)),
        "kv_latent_g": gain(ks[6], (L, KV_LORA)),
        "w_uq": nrm(ks[7], (L, Q_LORA, N_HEADS * QK_DIM), Q_LORA ** -0.5),
        "w_uk": nrm(ks[8], (L, KV_LORA, N_HEADS * NOPE_DIM), KV_LORA ** -0.5),
        "w_uv": nrm(ks[9], (L, KV_LORA, N_HEADS * V_DIM), KV_LORA ** -0.5),
        "q_norm_g": gain(ks[10], (L, QK_DIM)),
        "k_norm_g": gain(ks[11], (L, QK_DIM)),
        "conv_w": nrm(ks[12], (L, 3, (HYENA_ORDER + 1) * HYENA_WIDTH), 3 ** -0.5),
        "conv_b": nrm(ks[13], (L, (HYENA_ORDER + 1) * HYENA_WIDTH), 0.02),
        "filt_w1": nrm(ks[14], (L, FILTER_EMB, FILTER_HIDDEN), FILTER_EMB ** -0.5),
        "filt_b1": nrm(ks[15], (L, FILTER_HIDDEN), 0.1),
        "filt_freq1": gain(ks[16], (L, FILTER_HIDDEN)),
        "filt_w2": nrm(ks[17], (L, FILTER_HIDDEN, FILTER_HIDDEN), FILTER_HIDDEN ** -0.5),
        "filt_b2": nrm(ks[18], (L, FILTER_HIDDEN), 0.1),
        "filt_freq2": gain(ks[19], (L, FILTER_HIDDEN)),
        "filt_w3": nrm(ks[20], (L, FILTER_HIDDEN, HYENA_ORDER * 2 * HYENA_WIDTH), FILTER_HIDDEN ** -0.5),
        "hyena_d": nrm(ks[21], (L, HYENA_ORDER, HYENA_WIDTH), 1.0),
        "out_norm_g": gain(ks[22], (L, MIX_WIDTH)),
        "w_out": nrm(ks[23], (L, MIX_WIDTH, D_MODEL), MIX_WIDTH ** -0.5),
        "norm_ffn_g": gain(ks[24], (L, D_MODEL)),
        "w_router_group": nrm(ks[25], (L, D_MODEL, N_GROUPS), D_MODEL ** -0.5),
        "b_router_group": nrm(ks[26], (L, N_GROUPS), 0.01),
        "w_router_expert": nrm(ks[27], (L, D_MODEL, N_EXPERTS), D_MODEL ** -0.5),
        "b_router_expert": nrm(ks[28], (L, N_EXPERTS), 0.01),
        "w_gate": nrm(ks[29], (L, N_EXPERTS, D_MODEL, D_EXPERT), D_MODEL ** -0.5),
        "w_up": nrm(ks[30], (L, N_EXPERTS, D_MODEL, D_EXPERT), D_MODEL ** -0.5),
        "w_down": nrm(ks[31], (L, N_EXPERTS, D_EXPERT, D_MODEL), D_EXPERT ** -0.5),
    }


def reference(x_prompt, x_sample, meta_tokens, norm_mix_g, w_in, q_latent_g, kv_latent_g, w_uq, w_uk, w_uv,
              q_norm_g, k_norm_g, conv_w, conv_b, filt_w1, filt_b1, filt_freq1, filt_w2, filt_b2, filt_freq2,
              filt_w3, hyena_d, out_norm_g, w_out, norm_ffn_g, w_router_group, b_router_group,
              w_router_expert, b_router_expert, w_gate, w_up, w_down):
    params = dict(meta_tokens=meta_tokens, norm_mix_g=norm_mix_g, w_in=w_in, q_latent_g=q_latent_g,
                  kv_latent_g=kv_latent_g, w_uq=w_uq, w_uk=w_uk, w_uv=w_uv, q_norm_g=q_norm_g,
                  k_norm_g=k_norm_g, conv_w=conv_w, conv_b=conv_b, filt_w1=filt_w1, filt_b1=filt_b1,
                  filt_freq1=filt_freq1, filt_w2=filt_w2, filt_b2=filt_b2, filt_freq2=filt_freq2,
                  filt_w3=filt_w3, hyena_d=hyena_d, out_norm_g=out_norm_g, w_out=w_out,
                  norm_ffn_g=norm_ffn_g, w_router_group=w_router_group, b_router_group=b_router_group,
                  w_router_expert=w_router_expert, b_router_expert=b_router_expert,
                  w_gate=w_gate, w_up=w_up, w_down=w_down)
    y_prompt = trunk(x_prompt, params)
    y_sample = trunk(x_sample, params)
    return (y_prompt, y_sample)
```

```python
import dataclasses
import functools
import math

import jax
import jax.numpy as jnp
import numpy as np
from jax import lax
from jax.experimental import pallas as pl
from jax.experimental.pallas import tpu as pltpu

F32 = jnp.float32
BF16 = jnp.bfloat16
EPS = 1e-6
NEG_BIG = -1e30

LANE = 128
SUBLANE = 8
V7X_VMEM_BYTES = 64 * 1024 * 1024
VMEM_LIMIT = V7X_VMEM_BYTES - 8 * 1024 * 1024

ROW_TILE = 256
TIME_BLOCK = LANE


@dataclasses.dataclass(frozen=True)
class Dims:
    d_model: int = 4096
    n_meta: int = 16
    n_heads: int = 16
    nope: int = 128
    rope: int = 64
    v_dim: int = 128
    q_lora: int = 1024
    kv_lora: int = 512
    hy_width: int = 2048
    hy_order: int = 2
    out_group: int = 128
    filt_emb: int = 33
    filt_hidden: int = 64
    n_groups: int = 8
    epg: int = 8
    top_k: int = 2
    d_expert: int = 1024
    rope_theta: float = 10000.0
    fast_decay: float = 0.3
    slow_decay: float = 1.5
    decay_target: float = 1e-2
    moe_rows: int = 512
    moe_up_cols: int = 512
    moe_down_cols: int = 2048
    mm_rows: int = 512
    mm_cols: int = 1024
    dft_cols: int = 512

    @property
    def qk_dim(self):
        return self.nope + self.rope

    @property
    def head_pad(self):
        return _round_up(self.qk_dim, LANE)

    @property
    def n_experts(self):
        return self.n_groups * self.epg


@dataclasses.dataclass(frozen=True)
class Trunk:
    batch: int
    t: int
    tp: int
    row0: int

    @property
    def rows(self):
        return self.batch * self.tp

    @property
    def kh(self):
        return self.tp // TIME_BLOCK

    @property
    def n1(self):
        return _round_up(-(-(2 * self.t - 1) // TIME_BLOCK), SUBLANE)

    @property
    def n(self):
        return self.n1 * TIME_BLOCK


def _round_up(x, m):
    return -(-x // m) * m


def _cparams(sem):
    return pltpu.CompilerParams(dimension_semantics=sem, vmem_limit_bytes=VMEM_LIMIT)


def _dot(a, b):
    return jnp.dot(a, b, preferred_element_type=F32)


def _split_bf16(x):
    hi = x.astype(BF16)
    lo = (x - hi.astype(F32)).astype(BF16)
    return hi, lo


def _dot3(x, w_hi, w_lo):
    x_hi, x_lo = _split_bf16(x)
    return _dot(x_hi, w_hi) + _dot(x_hi, w_lo) + _dot(x_lo, w_hi)


def _seq_valid(rows, segments):
    ok = None
    for start, length in segments:
        m = (rows >= start) & (rows < start + length)
        ok = m if ok is None else (ok | m)
    return ok


def _rmsnorm_kernel(x_ref, g_ref, o_ref):
    x = x_ref[...]
    ms = jnp.mean(x * x, axis=-1, keepdims=True)
    o_ref[...] = (x * lax.rsqrt(ms + EPS) * g_ref[...]).astype(o_ref.dtype)


def rmsnorm(x, g, tm=ROW_TILE):
    r, d = x.shape
    return pl.pallas_call(
        _rmsnorm_kernel,
        out_shape=jax.ShapeDtypeStruct((r, d), BF16),
        grid=(r // tm,),
        in_specs=[pl.BlockSpec((tm, d), lambda i: (i, 0)), pl.BlockSpec((1, d), lambda i: (0, 0))],
        out_specs=pl.BlockSpec((tm, d), lambda i: (i, 0)),
        compiler_params=_cparams(("parallel",)),
        name="rmsnorm",
    )(x, g.reshape(1, d))


def _rmsnorm_router_kernel(x_ref, g_ref, whi_ref, wlo_ref, h_ref, lg_ref):
    x = x_ref[...]
    ms = jnp.mean(x * x, axis=-1, keepdims=True)
    h = x * lax.rsqrt(ms + EPS) * g_ref[...]
    h_ref[...] = h.astype(h_ref.dtype)
    lg_ref[...] = _dot3(h, whi_ref[...], wlo_ref[...])


def rmsnorm_router(x, g, w_router, tm=ROW_TILE):
    r, d = x.shape
    nc = _round_up(w_router.shape[1], LANE)
    w = jnp.pad(w_router.astype(F32), ((0, 0), (0, nc - w_router.shape[1])))
    w_hi, w_lo = _split_bf16(w)
    return pl.pallas_call(
        _rmsnorm_router_kernel,
        out_shape=(jax.ShapeDtypeStruct((r, d), F32), jax.ShapeDtypeStruct((r, nc), F32)),
        grid=(r // tm,),
        in_specs=[
            pl.BlockSpec((tm, d), lambda i: (i, 0)),
            pl.BlockSpec((1, d), lambda i: (0, 0)),
            pl.BlockSpec((d, nc), lambda i: (0, 0)),
            pl.BlockSpec((d, nc), lambda i: (0, 0)),
        ],
        out_specs=(pl.BlockSpec((tm, d), lambda i: (i, 0)), pl.BlockSpec((tm, nc), lambda i: (i, 0))),
        compiler_params=_cparams(("parallel",)),
        name="rmsnorm_router",
    )(x, g.reshape(1, d), w_hi, w_lo)


def _mm_kernel(a_ref, b_ref, o_ref):
    o_ref[...] = _dot(a_ref[...], b_ref[...]).astype(o_ref.dtype)


def matmul(a, b, out_dtype, tm, tn):
    m, k = a.shape
    n = b.shape[1]
    return pl.pallas_call(
        _mm_kernel,
        out_shape=jax.ShapeDtypeStruct((m, n), out_dtype),
        grid=(m // tm, n // tn),
        in_specs=[pl.BlockSpec((tm, k), lambda i, j: (i, 0)), pl.BlockSpec((k, tn), lambda i, j: (0, j))],
        out_specs=pl.BlockSpec((tm, tn), lambda i, j: (i, j)),
        compiler_params=_cparams(("parallel", "parallel")),
        name="matmul",
    )(a, b)


def _mm_residual_kernel(a_ref, b_ref, r_ref, o_ref, *, tm, segments):
    acc = _dot(a_ref[...], b_ref[...])
    rows = pl.program_id(0) * tm + lax.broadcasted_iota(jnp.int32, (tm, 1), 0)
    o_ref[...] = jnp.where(_seq_valid(rows, segments), r_ref[...] + acc, 0.0)


def matmul_residual(a, b, res, segments, tm, tn):
    m, k = a.shape
    n = b.shape[1]
    return pl.pallas_call(
        functools.partial(_mm_residual_kernel, tm=tm, segments=segments),
        out_shape=jax.ShapeDtypeStruct((m, n), F32),
        grid=(m // tm, n // tn),
        in_specs=[
            pl.BlockSpec((tm, k), lambda i, j: (i, 0)),
            pl.BlockSpec((k, tn), lambda i, j: (0, j)),
            pl.BlockSpec((tm, tn), lambda i, j: (i, j)),
        ],
        out_specs=pl.BlockSpec((tm, tn), lambda i, j: (i, j)),
        compiler_params=_cparams(("parallel", "parallel")),
        name="matmul_residual",
    )(a, b, res)


def _latent_kernel(h_ref, w_ref, gq_ref, gkv_ref, cq_ref, ckv_ref, kpe_ref, *, q_lora, kv_lora):
    p = _dot(h_ref[...], w_ref[...])
    q = p[:, :q_lora]
    kv = p[:, q_lora : q_lora + kv_lora]
    cq_ref[...] = (q * lax.rsqrt(jnp.mean(q * q, axis=-1, keepdims=True) + EPS) * gq_ref[...]).astype(cq_ref.dtype)
    ckv_ref[...] = (kv * lax.rsqrt(jnp.mean(kv * kv, axis=-1, keepdims=True) + EPS) * gkv_ref[...]).astype(
        ckv_ref.dtype
    )
    kpe_ref[...] = p[:, q_lora + kv_lora :]


def latent_proj(h, w_lat, gq, gkv, dm, tm=ROW_TILE):
    r, d = h.shape
    nl = w_lat.shape[1]
    return pl.pallas_call(
        functools.partial(_latent_kernel, q_lora=dm.q_lora, kv_lora=dm.kv_lora),
        out_shape=(
            jax.ShapeDtypeStruct((r, dm.q_lora), BF16),
            jax.ShapeDtypeStruct((r, dm.kv_lora), BF16),
            jax.ShapeDtypeStruct((r, LANE), F32),
        ),
        grid=(r // tm,),
        in_specs=[
            pl.BlockSpec((tm, d), lambda i: (i, 0)),
            pl.BlockSpec((d, nl), lambda i: (0, 0)),
            pl.BlockSpec((1, dm.q_lora), lambda i: (0, 0)),
            pl.BlockSpec((1, dm.kv_lora), lambda i: (0, 0)),
        ],
        out_specs=(
            pl.BlockSpec((tm, dm.q_lora), lambda i: (i, 0)),
            pl.BlockSpec((tm, dm.kv_lora), lambda i: (i, 0)),
            pl.BlockSpec((tm, LANE), lambda i: (i, 0)),
        ),
        compiler_params=_cparams(("parallel",)),
        name="latent_proj",
    )(h, w_lat, gq.reshape(1, -1), gkv.reshape(1, -1))


def _rope_tail(c, cos_t, sin_a, sin_b, half):
    return c * cos_t + pltpu.roll(c, LANE - half, 1) * sin_a + pltpu.roll(c, half, 1) * sin_b


def _qkv_kernel(cq_ref, ckv_ref, kpe_ref, cos_ref, sa_ref, sb_ref, wq_ref, wk_ref, wv_ref, gq_ref, gk_ref,
                q_ref, k_ref, v_ref, *, dm):
    hp, nope, qk = dm.head_pad, dm.nope, dm.qk_dim
    half = dm.rope // 2
    scale = qk ** -0.5
    ckv = ckv_ref[...]
    q = _dot(cq_ref[...], wq_ref[...])
    kn = _dot(ckv, wk_ref[...])
    v_ref[...] = _dot(ckv, wv_ref[...]).astype(v_ref.dtype)
    pe = kpe_ref[...]
    pe_ss = jnp.sum(pe * pe, axis=-1, keepdims=True)
    cos_t, sin_a, sin_b = cos_ref[...], sa_ref[...], sb_ref[...]
    gq = gq_ref[...]
    gk = gk_ref[...]
    for h in range(dm.n_heads):
        qh = q[:, h * hp : (h + 1) * hp]
        r = lax.rsqrt(jnp.sum(qh * qh, axis=-1, keepdims=True) * (1.0 / qk) + EPS) * scale
        qa = qh[:, :nope] * r * gq[:, :nope]
        qc = _rope_tail(qh[:, nope:] * r * gq[:, nope:], cos_t, sin_a, sin_b, half)
        q_ref[:, h * hp : h * hp + nope] = qa.astype(q_ref.dtype)
        q_ref[:, h * hp + nope : (h + 1) * hp] = qc.astype(q_ref.dtype)
        kh = kn[:, h * nope : (h + 1) * nope]
        rk = lax.rsqrt((jnp.sum(kh * kh, axis=-1, keepdims=True) + pe_ss) * (1.0 / qk) + EPS)
        ka = kh * rk * gk[:, :nope]
        kc = _rope_tail(pe * rk * gk[:, nope:], cos_t, sin_a, sin_b, half)
        k_ref[:, h * hp : h * hp + nope] = ka.astype(k_ref.dtype)
        k_ref[:, h * hp + nope : (h + 1) * hp] = kc.astype(k_ref.dtype)


def qkv_heads(cq, ckv, kpe, rope_tabs, wq, wk, wv, gq, gk, dm, tk, tm=ROW_TILE):
    off = tk.row0 // tm
    hp, nh = dm.head_pad, dm.n_heads
    row = lambda i: (off + i, 0)
    loc = lambda i: (i, 0)
    fixed = lambda i: (0, 0)
    return pl.pallas_call(
        functools.partial(_qkv_kernel, dm=dm),
        out_shape=(
            jax.ShapeDtypeStruct((tk.rows, nh * hp), BF16),
            jax.ShapeDtypeStruct((tk.rows, nh * hp), BF16),
            jax.ShapeDtypeStruct((tk.rows, nh * dm.v_dim), BF16),
        ),
        grid=(tk.rows // tm,),
        in_specs=[
            pl.BlockSpec((tm, dm.q_lora), row),
            pl.BlockSpec((tm, dm.kv_lora), row),
            pl.BlockSpec((tm, LANE), row),
            pl.BlockSpec((tm, LANE), loc),
            pl.BlockSpec((tm, LANE), loc),
            pl.BlockSpec((tm, LANE), loc),
            pl.BlockSpec(wq.shape, fixed),
            pl.BlockSpec(wk.shape, fixed),
            pl.BlockSpec(wv.shape, fixed),
            pl.BlockSpec((1, hp), fixed),
            pl.BlockSpec((1, hp), fixed),
        ],
        out_specs=(
            pl.BlockSpec((tm, nh * hp), loc),
            pl.BlockSpec((tm, nh * hp), loc),
            pl.BlockSpec((tm, nh * dm.v_dim), loc),
        ),
        compiler_params=_cparams(("parallel",)),
        name="qkv_heads",
    )(cq, ckv, kpe, *rope_tabs, wq, wk, wv, gq, gk)


def _attn_kernel(q_ref, k_ref, v_ref, o_ref, *, t, tk):
    q = q_ref[...]
    tq = q.shape[0]
    n_full, rem = divmod(t, tk)

    def chunk(c, carry, masked):
        m, l, acc = carry
        start = c * tk if isinstance(c, int) else pl.multiple_of(c * tk, tk)
        k = k_ref[pl.ds(start, tk), :]
        v = v_ref[pl.ds(start, tk), :]
        s = lax.dot_general(q, k, (((1,), (1,)), ((), ())), preferred_element_type=F32)
        if masked:
            col = lax.broadcasted_iota(jnp.int32, s.shape, 1)
            s = jnp.where(col < rem, s, NEG_BIG)
        m_new = jnp.maximum(m, jnp.max(s, axis=-1, keepdims=True))
        alpha = jnp.exp(m - m_new)
        p = jnp.exp(s - m_new)
        l = alpha * l + jnp.sum(p, axis=-1, keepdims=True)
        acc = alpha * acc + _dot(p.astype(v.dtype), v)
        return m_new, l, acc

    init = (jnp.full((tq, 1), NEG_BIG, F32), jnp.zeros((tq, 1), F32), jnp.zeros((tq, v_ref.shape[1]), F32))
    carry = lax.fori_loop(0, n_full, lambda c, cr: chunk(c, cr, False), init)
    if rem:
        carry = chunk(n_full, carry, True)
    _, l, acc = carry
    o_ref[...] = (acc / l).astype(o_ref.dtype)


def attention(q, k, v, dm, tk, tq=ROW_TILE, tkc=ROW_TILE):
    nq = tk.tp // tq
    hp = dm.head_pad
    return pl.pallas_call(
        functools.partial(_attn_kernel, t=tk.t, tk=tkc),
        out_shape=jax.ShapeDtypeStruct((tk.rows, dm.n_heads * dm.v_dim), F32),
        grid=(tk.batch, dm.n_heads, nq),
        in_specs=[
            pl.BlockSpec((tq, hp), lambda b, h, i: (b * nq + i, h)),
            pl.BlockSpec((tk.tp, hp), lambda b, h, i: (b, h)),
            pl.BlockSpec((tk.tp, dm.v_dim), lambda b, h, i: (b, h)),
        ],
        out_specs=pl.BlockSpec((tq, dm.v_dim), lambda b, h, i: (b * nq + i, h)),
        compiler_params=_cparams(("parallel", "parallel", "arbitrary")),
        name="attention",
    )(q, k, v)


def _group_norm_kernel(a_ref, hy_ref, g_ref, o_ref, *, group):
    wa = a_ref.shape[1]
    g = g_ref[...]
    for src, base in ((a_ref, 0), (hy_ref, wa)):
        for c in range(src.shape[1] // group):
            xc = src[:, c * group : (c + 1) * group]
            y = xc * lax.rsqrt(jnp.mean(xc * xc, axis=-1, keepdims=True) + EPS)
            lo = base + c * group
            o_ref[:, lo : lo + group] = (y * g[:, lo : lo + group]).astype(o_ref.dtype)


def group_norm(a, hy, g, group, tm=ROW_TILE):
    r, wa = a.shape
    wh = hy.shape[1]
    return pl.pallas_call(
        functools.partial(_group_norm_kernel, group=group),
        out_shape=jax.ShapeDtypeStruct((r, wa + wh), BF16),
        grid=(r // tm,),
        in_specs=[
            pl.BlockSpec((tm, wa), lambda i: (i, 0)),
            pl.BlockSpec((tm, wh), lambda i: (i, 0)),
            pl.BlockSpec((1, wa + wh), lambda i: (0, 0)),
        ],
        out_specs=pl.BlockSpec((tm, wa + wh), lambda i: (i, 0)),
        compiler_params=_cparams(("parallel",)),
        name="group_norm",
    )(a, hy, g.reshape(1, -1))


def _conv3_kernel(u_ref, prev_ref, next_ref, w_ref, b_ref, o_ref, *, tm, t, tp, starts_at_row0):
    i = pl.program_id(0)
    u = u_ref[...]
    r = lax.broadcasted_iota(jnp.int32, (tm, 1), 0)
    prev_row = prev_ref[SUBLANE - 1 : SUBLANE, :]
    if starts_at_row0:
        prev_row = jnp.where(i == 0, 0.0, prev_row)
    up = jnp.where(r == 0, prev_row, pltpu.roll(u, 1, 0))
    dn = jnp.where(r == tm - 1, next_ref[0:1, :], pltpu.roll(u, tm - 1, 0))
    w = w_ref[...]
    uc = up * w[0:1, :] + u * w[1:2, :] + dn * w[2:3, :] + b_ref[...]
    pos = (i % (tp // tm)) * tm + r
    o_ref[...] = jnp.where(pos < t, uc, 0.0)


def conv3(u, w, b, tk, n_split, tm=ROW_TILE):
    cols = u.shape[1]
    tc = cols // n_split
    off = tk.row0 // tm
    sub = tm // SUBLANE
    last = u.shape[0] // SUBLANE - 1
    return pl.pallas_call(
        functools.partial(_conv3_kernel, tm=tm, t=tk.t, tp=tk.tp, starts_at_row0=(tk.row0 == 0)),
        out_shape=jax.ShapeDtypeStruct((n_split, tk.rows, tc), F32),
        grid=(tk.rows // tm, n_split),
        in_specs=[
            pl.BlockSpec((tm, tc), lambda i, j: (off + i, j)),
            pl.BlockSpec((SUBLANE, tc), lambda i, j: (jnp.maximum((off + i) * sub - 1, 0), j)),
            pl.BlockSpec((SUBLANE, tc), lambda i, j: (jnp.minimum((off + i + 1) * sub, last), j)),
            pl.BlockSpec((3, tc), lambda i, j: (0, j)),
            pl.BlockSpec((1, tc), lambda i, j: (0, j)),
        ],
        out_specs=pl.BlockSpec((None, tm, tc), lambda i, j: (j, i, 0)),
        compiler_params=_cparams(("parallel", "parallel")),
        name="conv3",
    )(u, u, u, w, b.reshape(1, -1))


def _taps_kernel(fvec_ref, w1h_ref, w1l_ref, b1_ref, f1_ref, w2h_ref, w2l_ref, b2_ref, f2_ref, w3h_ref, w3l_ref,
                 dl_ref, o_ref, *, tr, t, n, n_orders, bands):
    p = pl.program_id(0) * tr + lax.broadcasted_iota(jnp.int32, (tr, 1), 0)
    fwd = p < t
    bwd = p > n - t
    lag = jnp.where(fwd | bwd, jnp.where(fwd, p, n - p), 0).astype(F32)
    t_norm = lag * (1.0 / (t - 1))
    arg = ((2.0 * math.pi) * lag / t) * fvec_ref[...]
    lane = lax.broadcasted_iota(jnp.int32, (1, LANE), 1)
    cos_l = (lane >= 1) & (lane <= bands)
    sin_l = (lane > bands) & (lane <= 2 * bands)
    feats = jnp.where(lane == 0, t_norm, jnp.where(cos_l, jnp.cos(arg), jnp.where(sin_l, -jnp.sin(arg), 0.0)))
    hid = jnp.sin(f1_ref[...] * (_dot3(feats, w1h_ref[...], w1l_ref[...]) + b1_ref[...]))
    hid = jnp.sin(f2_ref[...] * (_dot3(hid, w2h_ref[...], w2l_ref[...]) + b2_ref[...]))
    decay = jnp.exp(-t_norm * dl_ref[...])
    for o in range(n_orders):
        f_fwd = _dot3(hid, w3h_ref[2 * o], w3l_ref[2 * o])
        f_bwd = _dot3(hid, w3h_ref[2 * o + 1], w3l_ref[2 * o + 1])
        o_ref[o] = jnp.where(fwd, f_fwd, jnp.where(bwd, f_bwd, 0.0)) * decay


def hyena_taps(fp, tk, dm, tr=ROW_TILE, tc=512):
    n = tk.n
    c = dm.hy_width
    tc = min(tc, c)
    no = dm.hy_order
    fixed2 = lambda i, j: (0, 0)
    sq = pl.BlockSpec((LANE, LANE), fixed2)
    vec = pl.BlockSpec((1, LANE), fixed2)
    return pl.pallas_call(
        functools.partial(_taps_kernel, tr=tr, t=tk.t, n=n, n_orders=no, bands=(dm.filt_emb - 1) // 2),
        out_shape=jax.ShapeDtypeStruct((no, n, c), F32),
        grid=(n // tr, c // tc),
        in_specs=[vec, sq, sq, vec, vec, sq, sq, vec, vec,
                  pl.BlockSpec((2 * no, LANE, tc), lambda i, j: (0, 0, j)),
                  pl.BlockSpec((2 * no, LANE, tc), lambda i, j: (0, 0, j)),
                  pl.BlockSpec((1, tc), lambda i, j: (0, j))],
        out_specs=pl.BlockSpec((no, tr, tc), lambda i, j: (0, i, j)),
        compiler_params=_cparams(("parallel", "parallel")),
        name="hyena_taps",
    )(fp["fvec"], fp["w1h"], fp["w1l"], fp["b1"], fp["f1"], fp["w2h"], fp["w2l"], fp["b2"], fp["f2"], fp["w3h"],
      fp["w3l"], fp["deltas"])


def _pdft_fwd_kernel(g_ref, z_ref, ar_ref, ai_ref, xs_ref, *, parts, rows, n1):
    xs_ref[...] = jnp.zeros_like(xs_ref)
    half = xs_ref.shape[0] // parts
    for p in range(parts):
        xs_ref[p * half : p * half + rows, :] = z_ref[p]
    res = _dot(g_ref[...], xs_ref[...].astype(BF16))
    ar_ref[...] = res[:n1].astype(ar_ref.dtype)
    ai_ref[...] = res[n1:].astype(ai_ref.dtype)


def pdft_fwd(g, z, n1, which):
    s0, sn = which
    _, parts, rows, wide = z.shape
    c = wide // TIME_BLOCK
    kpad = g.shape[2]
    out = jax.ShapeDtypeStruct((sn, n1, wide), BF16)
    return pl.pallas_call(
        functools.partial(_pdft_fwd_kernel, parts=parts, rows=rows, n1=n1),
        out_shape=(out, out),
        grid=(sn, TIME_BLOCK),
        in_specs=[
            pl.BlockSpec((None, 2 * n1, kpad), lambda s, j: (j, 0, 0)),
            pl.BlockSpec((None, parts, rows, c), lambda s, j: (s0 + s, 0, 0, j)),
        ],
        out_specs=(
            pl.BlockSpec((None, n1, c), lambda s, j: (s, 0, j)),
            pl.BlockSpec((None, n1, c), lambda s, j: (s, 0, j)),
        ),
        scratch_shapes=[pltpu.VMEM((kpad, c), F32)],
        compiler_params=_cparams(("parallel", "parallel")),
        name="pdft_fwd",
    )(g, z)


def _spectrum_kernel(ar_ref, ai_ref, f_ref, h_ref):
    f = f_ref[...]
    for k in range(ar_ref.shape[0]):
        a = jnp.concatenate([ar_ref[k], ai_ref[k]], axis=0)
        h_ref[k] = _dot(f, a)


def filter_spectrum(ar, ai, f2f, dm, kb=SUBLANE):
    no, n1, wide = ar.shape
    c = wide // TIME_BLOCK
    cb = min(dm.dft_cols, c)
    a3 = (no, n1, TIME_BLOCK, c)
    blk = pl.BlockSpec((None, kb, TIME_BLOCK, cb), lambda o, i, j: (o, i, 0, j))
    return pl.pallas_call(
        _spectrum_kernel,
        out_shape=jax.ShapeDtypeStruct((no, n1, 2 * TIME_BLOCK, c), F32),
        grid=(no, n1 // kb, c // cb),
        in_specs=[blk, blk, pl.BlockSpec(f2f.shape, lambda o, i, j: (0, 0))],
        out_specs=pl.BlockSpec((None, kb, 2 * TIME_BLOCK, cb), lambda o, i, j: (o, i, 0, j)),
        compiler_params=_cparams(("parallel", "parallel", "parallel")),
        name="filter_spectrum",
    )(ar.reshape(a3), ai.reshape(a3), f2f)


def _freq_mul_kernel(ar_ref, ai_ref, h_ref, ff_ref, fi_ref, br_ref, bi_ref):
    ff = ff_ref[...]
    fi = fi_ref[...]
    nb = TIME_BLOCK
    for k in range(ar_ref.shape[0]):
        x = _dot(ff, jnp.concatenate([ar_ref[k], ai_ref[k]], axis=0))
        h = h_ref[k]
        xr, xi = x[:nb], x[nb:]
        hr, hi = h[:nb], h[nb:]
        y = jnp.concatenate([xr * hr - xi * hi, xr * hi + xi * hr], axis=0).astype(BF16)
        b = _dot(fi, y)
        br_ref[k] = b[:nb].astype(br_ref.dtype)
        bi_ref[k] = b[nb:].astype(bi_ref.dtype)


def freq_multiply(ar, ai, h, order, f2f, f2i, dm, kb=SUBLANE):
    _, n1, wide = ar.shape
    c = wide // TIME_BLOCK
    cb = min(dm.dft_cols, c)
    a3 = (n1, TIME_BLOCK, c)
    blk = pl.BlockSpec((kb, TIME_BLOCK, cb), lambda i, j: (i, 0, j))
    out = jax.ShapeDtypeStruct(a3, BF16)
    br, bi = pl.pallas_call(
        _freq_mul_kernel,
        out_shape=(out, out),
        grid=(n1 // kb, c // cb),
        in_specs=[blk, blk,
                  pl.BlockSpec((None, kb, 2 * TIME_BLOCK, cb), lambda i, j: (order, i, 0, j)),
                  pl.BlockSpec(f2f.shape, lambda i, j: (0, 0)),
                  pl.BlockSpec(f2i.shape, lambda i, j: (0, 0))],
        out_specs=(blk, blk),
        compiler_params=_cparams(("parallel", "parallel")),
        name="freq_multiply",
    )(ar.reshape(a3), ai.reshape(a3), h, f2f, f2i)
    return br.reshape(n1, wide), bi.reshape(n1, wide)


def _pdft_inv_gate_kernel(g_ref, br_ref, bi_ref, gate_ref, z_ref, d_ref, o_ref, bs_ref, *, n1, rows):
    bs_ref[...] = jnp.zeros_like(bs_ref)
    half = bs_ref.shape[0] // 2
    bs_ref[0:n1, :] = br_ref[...].astype(F32)
    bs_ref[half : half + n1, :] = bi_ref[...].astype(F32)
    y = _dot(g_ref[...], bs_ref[...].astype(BF16))
    mh = g_ref.shape[0] // 2
    d = d_ref[...]
    for b in range(2):
        o_ref[b] = gate_ref[b] * (y[b * mh : b * mh + rows] + d * z_ref[b])


def pdft_inv_gate(ginv, br, bi, sig, gate_idx, z, z_idx, d, tk):
    n1, wide = br.shape
    c = wide // TIME_BLOCK
    kh = tk.kh
    kpad = ginv.shape[2]
    seq = lambda idx: pl.BlockSpec((None, 2, kh, c), lambda j: (idx, 0, 0, j))
    return pl.pallas_call(
        functools.partial(_pdft_inv_gate_kernel, n1=n1, rows=kh),
        out_shape=jax.ShapeDtypeStruct((1, 2, kh, wide), F32),
        grid=(TIME_BLOCK,),
        in_specs=[
            pl.BlockSpec((None, ginv.shape[1], kpad), lambda j: (j, 0, 0)),
            pl.BlockSpec((n1, c), lambda j: (0, j)),
            pl.BlockSpec((n1, c), lambda j: (0, j)),
            seq(gate_idx),
            seq(z_idx),
            pl.BlockSpec((1, c), lambda j: (0, 0)),
        ],
        out_specs=pl.BlockSpec((None, 2, kh, c), lambda j: (0, 0, 0, j)),
        scratch_shapes=[pltpu.VMEM((kpad, c), F32)],
        compiler_params=_cparams(("parallel",)),
        name="pdft_inv_gate",
    )(ginv, br, bi, sig, z, d.reshape(1, c))


def _dft_real_form(phase_num, n, sign, scale=1.0):
    ang = (sign * 2.0 * math.pi / n) * (phase_num % n).astype(F32)
    c = jnp.cos(ang) * scale
    s = jnp.sin(ang) * scale
    top = jnp.concatenate([c, -s], axis=-1)
    bot = jnp.concatenate([s, c], axis=-1)
    return jnp.concatenate([top, bot], axis=-2).astype(BF16)


def dft_tables(tk):
    n1, n, nb = tk.n1, tk.n, TIME_BLOCK
    i32 = jnp.int32
    n2 = jnp.arange(nb, dtype=i32)[:, None, None]
    k1 = jnp.arange(n1, dtype=i32)[None, :, None]

    def major(cols):
        col = jnp.arange(cols, dtype=i32)[None, None, :]
        return k1 * (nb * col + n2)

    sig_half = _round_up(tk.kh, 64)
    tap_half = _round_up(n1, LANE)
    out_half = _round_up(tk.kh, SUBLANE)
    g_sig = _dft_real_form(major(sig_half), n, -1.0)
    g_tap = _dft_real_form(major(tap_half), n, -1.0)[:, :, :tap_half]
    rows = jnp.arange(out_half, dtype=i32)[None, :, None]
    kcol = jnp.arange(tap_half, dtype=i32)[None, None, :]
    g_inv = _dft_real_form(kcol * (nb * rows + n2), n, 1.0, 1.0 / n)
    kk = jnp.arange(nb, dtype=i32)
    minor = kk[:, None] * kk[None, :]
    f2f = _dft_real_form(minor, nb, -1.0)
    f2i = _dft_real_form(minor, nb, 1.0)
    return dict(g_sig=g_sig, g_tap=g_tap, g_inv=g_inv, f2f=f2f, f2i=f2i)


def filter_params(p, l, dm):
    emb, hid, c, no = dm.filt_emb, dm.filt_hidden, dm.hy_width, dm.hy_order
    bands = (emb - 1) // 2

    def pad2(w, r, cc):
        return jnp.pad(w.astype(F32), ((0, r - w.shape[0]), (0, cc - w.shape[1])))

    def padv(v):
        return jnp.pad(v.astype(F32), (0, LANE - v.shape[0])).reshape(1, LANE)

    freqs = jnp.linspace(1e-4, bands - 1, bands, dtype=F32)
    fvec = jnp.concatenate([jnp.zeros((1,), F32), freqs, freqs, jnp.zeros((LANE - emb,), F32)]).reshape(1, LANE)
    w1h, w1l = _split_bf16(pad2(p["filt_w1"][l], LANE, LANE))
    w2h, w2l = _split_bf16(pad2(p["filt_w2"][l], LANE, LANE))
    w3 = p["filt_w3"][l].astype(F32).reshape(hid, 2 * no, c).transpose(1, 0, 2)
    w3h, w3l = _split_bf16(jnp.pad(w3, ((0, 0), (0, LANE - hid), (0, 0))))
    deltas = jnp.abs(jnp.linspace(math.log(dm.decay_target) / dm.slow_decay,
                                  math.log(dm.decay_target) / dm.fast_decay, c, dtype=F32)).reshape(1, c)
    return dict(fvec=fvec, w1h=w1h, w1l=w1l, b1=padv(p["filt_b1"][l]), f1=padv(p["filt_freq1"][l]),
                w2h=w2h, w2l=w2l, b2=padv(p["filt_b2"][l]), f2=padv(p["filt_freq2"][l]),
                w3h=w3h, w3l=w3l, deltas=deltas)


def hyena(u, p, l, tk, tabs, dm):
    c, no = dm.hy_width, dm.hy_order
    n1, kh = tk.n1, tk.kh
    assert tk.batch == 2, "the two sequences of a trunk are the real and imaginary parts of one transform"
    sig = conv3(u, p["conv_w"][l], p["conv_b"][l], tk, no + 1)
    sig = sig.reshape(no + 1, tk.batch, kh, TIME_BLOCK * c)
    taps = hyena_taps(filter_params(p, l, dm), tk, dm)
    tr, ti = pdft_fwd(tabs["g_tap"], taps.reshape(no, 1, n1, TIME_BLOCK * c), n1, (0, no))
    spec = filter_spectrum(tr, ti, tabs["f2f"], dm)
    z, z_idx = sig, 0
    for o in range(no):
        ar, ai = pdft_fwd(tabs["g_sig"], z, n1, (z_idx, 1))
        br, bi = freq_multiply(ar, ai, spec, o, tabs["f2f"], tabs["f2i"], dm)
        z = pdft_inv_gate(tabs["g_inv"], br, bi, sig, o + 1, z, z_idx, p["hyena_d"][l, o], tk)
        z_idx = 0
    return z.reshape(tk.rows, c)


def _row_copy(src_ref, src_row, dst_ref, dst_row, sem):
    return pltpu.make_async_copy(src_ref.at[pl.ds(src_row, 1)], dst_ref.at[pl.ds(dst_row, 1)], sem)


def _gather_rows_kernel(idx_ref, src_ref, o_ref, buf_ref, sem, *, tg):
    def issue(i, _):
        _row_copy(src_ref, idx_ref[i], buf_ref, i, sem).start()
        return 0

    lax.fori_loop(0, tg, issue, 0)

    def drain(i, _):
        _row_copy(src_ref, 0, buf_ref, i, sem).wait()
        return 0

    lax.fori_loop(0, tg, drain, 0)
    o_ref[...] = buf_ref[...].astype(o_ref.dtype)


def gather_rows(src, idx, out_dtype, tg=ROW_TILE):
    n = idx.shape[0]
    d = src.shape[1]
    return pl.pallas_call(
        functools.partial(_gather_rows_kernel, tg=tg),
        out_shape=jax.ShapeDtypeStruct((n, d), out_dtype),
        grid=(n // tg,),
        in_specs=[
            pl.BlockSpec((tg,), lambda i: (i,), memory_space=pltpu.SMEM),
            pl.BlockSpec(memory_space=pl.ANY),
        ],
        out_specs=pl.BlockSpec((tg, d), lambda i: (i, 0)),
        scratch_shapes=[pltpu.VMEM((tg, d), src.dtype), pltpu.SemaphoreType.DMA],
        compiler_params=_cparams(("arbitrary",)),
        name="moe_gather",
    )(idx, src)


def _expert_up_kernel(be_ref, first_ref, nb_ref, x_ref, wg_ref, wu_ref, o_ref, wgs_ref, wus_ref):
    m = pl.program_id(1)

    @pl.when(m < nb_ref[0])
    def _():
        @pl.when(first_ref[m] == 1)
        def _():
            wgs_ref[...] = wg_ref[...].astype(BF16)
            wus_ref[...] = wu_ref[...].astype(BF16)

        x = x_ref[...]
        g = _dot(x, wgs_ref[...])
        u = _dot(x, wus_ref[...])
        o_ref[...] = (g * jax.nn.sigmoid(g) * u).astype(o_ref.dtype)

    @pl.when(m >= nb_ref[0])
    def _():
        o_ref[...] = jnp.zeros_like(o_ref)


def expert_up(xs, w_gate, w_up, l, blk_expert, blk_first, n_blocks, dm):
    cap, d = xs.shape
    tm, tf = dm.moe_rows, min(dm.moe_up_cols, dm.d_expert)
    nblk = cap // tm
    wspec = pl.BlockSpec((None, None, d, tf), lambda j, m, be, fi, nb: (l, be[m], 0, j))
    return pl.pallas_call(
        _expert_up_kernel,
        out_shape=jax.ShapeDtypeStruct((cap, dm.d_expert), BF16),
        grid_spec=pltpu.PrefetchScalarGridSpec(
            num_scalar_prefetch=3,
            grid=(dm.d_expert // tf, nblk),
            in_specs=[pl.BlockSpec((tm, d), lambda j, m, be, fi, nb: (m, 0)), wspec, wspec],
            out_specs=pl.BlockSpec((tm, tf), lambda j, m, be, fi, nb: (m, j)),
            scratch_shapes=[pltpu.VMEM((d, tf), BF16), pltpu.VMEM((d, tf), BF16)],
        ),
        compiler_params=_cparams(("arbitrary", "arbitrary")),
        name="moe_expert_up",
    )(blk_expert, blk_first, n_blocks, xs, w_gate, w_up)


def _expert_down_kernel(be_ref, first_ref, nb_ref, h_ref, wd_ref, o_ref, wds_ref):
    m = pl.program_id(1)

    @pl.when(m < nb_ref[0])
    def _():
        @pl.when(first_ref[m] == 1)
        def _():
            wds_ref[...] = wd_ref[...].astype(BF16)

        o_ref[...] = _dot(h_ref[...], wds_ref[...]).astype(o_ref.dtype)

    @pl.when(m >= nb_ref[0])
    def _():
        o_ref[...] = jnp.zeros_like(o_ref)


def expert_down(hs, w_down, l, blk_expert, blk_first, n_blocks, dm):
    cap, f = hs.shape
    d = w_down.shape[3]
    tm, tn = dm.moe_rows, min(dm.moe_down_cols, d)
    nblk = cap // tm
    return pl.pallas_call(
        _expert_down_kernel,
        out_shape=jax.ShapeDtypeStruct((cap, d), F32),
        grid_spec=pltpu.PrefetchScalarGridSpec(
            num_scalar_prefetch=3,
            grid=(d // tn, nblk),
            in_specs=[pl.BlockSpec((tm, f), lambda j, m, be, fi, nb: (m, 0)),
                      pl.BlockSpec((None, None, f, tn), lambda j, m, be, fi, nb: (l, be[m], 0, j))],
            out_specs=pl.BlockSpec((tm, tn), lambda j, m, be, fi, nb: (m, j)),
            scratch_shapes=[pltpu.VMEM((f, tn), BF16)],
        ),
        compiler_params=_cparams(("arbitrary", "arbitrary")),
        name="moe_expert_down",
    )(blk_expert, blk_first, n_blocks, hs, w_down)


def _combine_kernel(pos_ref, x_ref, w_ref, ys_ref, o_ref, buf_ref, sem, *, tm, top_k):
    def issue(i, _):
        for k in range(top_k):
            pltpu.make_async_copy(ys_ref.at[pl.ds(pos_ref[i * top_k + k], 1)], buf_ref.at[k, pl.ds(i, 1)], sem).start()
        return 0

    lax.fori_loop(0, tm, issue, 0)

    def drain(i, _):
        for k in range(top_k):
            pltpu.make_async_copy(ys_ref.at[pl.ds(0, 1)], buf_ref.at[k, pl.ds(i, 1)], sem).wait()
        return 0

    lax.fori_loop(0, tm, drain, 0)
    w = w_ref[...]
    acc = x_ref[...]
    for k in range(top_k):
        acc = acc + w[:, k : k + 1] * buf_ref[k].astype(F32)
    o_ref[...] = acc


def moe_combine(x, ys, pos, wts, dm, tm=ROW_TILE):
    r, d = x.shape
    k = dm.top_k
    return pl.pallas_call(
        functools.partial(_combine_kernel, tm=tm, top_k=k),
        out_shape=jax.ShapeDtypeStruct((r, d), F32),
        grid=(r // tm,),
        in_specs=[
            pl.BlockSpec((tm * k,), lambda i: (i,), memory_space=pltpu.SMEM),
            pl.BlockSpec((tm, d), lambda i: (i, 0)),
            pl.BlockSpec((tm, k), lambda i: (i, 0)),
            pl.BlockSpec(memory_space=pl.ANY),
        ],
        out_specs=pl.BlockSpec((tm, d), lambda i: (i, 0)),
        scratch_shapes=[pltpu.VMEM((k, tm, d), ys.dtype), pltpu.SemaphoreType.DMA],
        compiler_params=_cparams(("arbitrary",)),
        name="moe_combine",
    )(pos.reshape(-1), x, wts, ys)


def route(logits, b_group, b_expert, dm):
    ng, epg = dm.n_groups, dm.epg
    n = logits.shape[0]
    g_prob = jax.nn.softmax(logits[:, :ng] + b_group.astype(F32), axis=-1)
    g_top, g_idx = lax.top_k(g_prob, 1)
    e_logits = (logits[:, ng : ng + ng * epg] + b_expert.astype(F32)).reshape(n, ng, epg)
    in_group = jnp.take_along_axis(e_logits, g_idx[:, :, None], axis=1)[:, 0]
    e_top, e_idx = lax.top_k(in_group, dm.top_k)
    weights = g_top * jax.nn.softmax(e_top, axis=-1)
    return (g_idx * epg + e_idx).astype(jnp.int32), weights


def dispatch_plan(expert_ids, weights, valid_rows, n_rows, dm):
    ne, k, tm = dm.n_experts, dm.top_k, dm.moe_rows
    i32 = jnp.int32
    n_tok = valid_rows.shape[0]
    n_slots = n_tok * k
    vr = jnp.asarray(valid_rows, i32)
    flat_e = expert_ids[vr].reshape(-1)
    onehot = (flat_e[:, None] == jnp.arange(ne, dtype=i32)[None, :]).astype(i32)
    counts = jnp.sum(onehot, axis=0)
    rank = jnp.take_along_axis(jnp.cumsum(onehot, axis=0), flat_e[:, None], axis=1)[:, 0] - 1
    padded = (counts + tm - 1) // tm * tm
    pend = jnp.cumsum(padded)
    pstart = pend - padded
    dest = pstart[flat_e] + rank
    nblk = -(-n_slots // tm) + ne
    cap = nblk * tm
    slot_row = jnp.repeat(vr, k)
    row_of = jnp.zeros((cap,), i32).at[dest].set(slot_row)
    n_used = (pend[-1] // tm).astype(i32)
    blk_start = jnp.arange(nblk, dtype=i32) * tm
    be = jnp.minimum(jnp.searchsorted(pend, blk_start, side="right"), ne - 1).astype(i32)
    last_used = be[jnp.maximum(n_used - 1, 0)]
    be = jnp.where(jnp.arange(nblk) < n_used, be, last_used)
    first = jnp.concatenate([jnp.ones((1,), i32), (be[1:] != be[:-1]).astype(i32)])
    pos = jnp.zeros((n_rows, k), i32).at[vr].set(dest.reshape(n_tok, k))
    wts = jnp.zeros((n_rows, k), F32).at[vr].set(weights[vr].astype(F32))
    return dict(row_of=row_of, blk_expert=be, blk_first=first, n_used=n_used.reshape(1), pos=pos, wts=wts)


def moe_ffn(x, h, logits, p, l, valid_rows, dm):
    ids, weights = route(logits, p["b_router_group"][l], p["b_router_expert"][l], dm)
    plan = dispatch_plan(ids, weights, valid_rows, x.shape[0], dm)
    xs = gather_rows(h, plan["row_of"], BF16)
    hs = expert_up(xs, p["w_gate"], p["w_up"], l, plan["blk_expert"], plan["blk_first"], plan["n_used"], dm)
    ys = expert_down(hs, p["w_down"], l, plan["blk_expert"], plan["blk_first"], plan["n_used"], dm)
    return moe_combine(x, ys, plan["pos"], plan["wts"], dm)


def rope_tables(tk, dm):
    half = dm.rope // 2
    pos = jnp.arange(tk.tp, dtype=F32)
    inv = dm.rope_theta ** (-jnp.arange(half, dtype=F32) / half)
    ang = pos[:, None] * inv[None, :]
    cos, sin = jnp.cos(ang), jnp.sin(ang)
    z = jnp.zeros((tk.tp, LANE - 2 * half), F32)
    zh = jnp.zeros((tk.tp, half), F32)
    tabs = (jnp.concatenate([cos, cos, z], axis=1),
            jnp.concatenate([-sin, zh, z], axis=1),
            jnp.concatenate([zh, sin, z], axis=1))
    return tuple(jnp.tile(t, (tk.batch, 1)) for t in tabs)


def layer_weights(p, l, dm):
    o1, o2 = dm.q_lora, dm.q_lora + dm.kv_lora
    o3 = o2 + dm.rope
    w_in = p["w_in"][l]
    w_lat = jnp.pad(w_in[:, :o3], ((0, 0), (0, LANE - dm.rope))).astype(BF16)
    w_hy = w_in[:, o3:].astype(BF16)
    nh, qk, hp = dm.n_heads, dm.qk_dim, dm.head_pad
    wq = jnp.pad(p["w_uq"][l].reshape(dm.q_lora, nh, qk), ((0, 0), (0, 0), (0, hp - qk)))
    wq = wq.reshape(dm.q_lora, nh * hp).astype(BF16)
    gq = jnp.pad(p["q_norm_g"][l].astype(F32), (0, hp - qk)).reshape(1, hp)
    gk = jnp.pad(p["k_norm_g"][l].astype(F32), (0, hp - qk)).reshape(1, hp)
    w_router = jnp.concatenate([p["w_router_group"][l], p["w_router_expert"][l]], axis=1)
    return dict(w_lat=w_lat, w_hy=w_hy, wq=wq, wk=p["w_uk"][l].astype(BF16), wv=p["w_uv"][l].astype(BF16),
                gq=gq, gk=gk, w_out=p["w_out"][l].astype(BF16), w_router=w_router)


def encoder(x_list, p, dm):
    d, nm = dm.d_model, dm.n_meta
    depth = p["norm_mix_g"].shape[0]
    trunks, segments, row0 = [], [], 0
    for x in x_list:
        b, ln, _ = x.shape
        t = ln + nm
        tp = _round_up(t + 1, ROW_TILE)
        trunks.append(Trunk(batch=b, t=t, tp=tp, row0=row0))
        segments += [(row0 + s * tp, t) for s in range(b)]
        row0 += b * tp
    segments = tuple(segments)
    n_rows = row0
    valid_rows = np.concatenate([np.arange(s, s + ln) for s, ln in segments]).astype(np.int32)

    meta = p["meta_tokens"].astype(F32)
    pieces = []
    for x, tk in zip(x_list, trunks):
        mt = jnp.broadcast_to(meta[None], (tk.batch, nm, d))
        pad = jnp.zeros((tk.batch, tk.tp - tk.t, d), F32)
        pieces.append(jnp.concatenate([mt, x.astype(F32), pad], axis=1).reshape(tk.rows, d))
    x = jnp.concatenate(pieces, axis=0)

    ropes = [rope_tables(tk, dm) for tk in trunks]
    dfts = [dft_tables(tk) for tk in trunks]
    tm, tn = dm.mm_rows, dm.mm_cols
    for l in range(depth):
        w = layer_weights(p, l, dm)
        h = rmsnorm(x, p["norm_mix_g"][l])
        cq, ckv, kpe = latent_proj(h, w["w_lat"], p["q_latent_g"][l], p["kv_latent_g"][l], dm)
        u = matmul(h, w["w_hy"], F32, tm, min(tn, w["w_hy"].shape[1]))
        mixed = []
        for tk, rt, tabs in zip(trunks, ropes, dfts):
            q, k, v = qkv_heads(cq, ckv, kpe, rt, w["wq"], w["wk"], w["wv"], w["gq"], w["gk"], dm, tk)
            a = attention(q, k, v, dm, tk)
            hy = hyena(u, p, l, tk, tabs, dm)
            mixed.append(group_norm(a, hy, p["out_norm_g"][l], dm.out_group))
        mixed = jnp.concatenate(mixed, axis=0)
        x = matmul_residual(mixed, w["w_out"], x, segments, tm, min(tn, d))
        h, logits = rmsnorm_router(x, p["norm_ffn_g"][l], w["w_router"])
        x = moe_ffn(x, h, logits, p, l, valid_rows, dm)

    outs = []
    for xin, tk in zip(x_list, trunks):
        y = x[tk.row0 : tk.row0 + tk.rows].reshape(tk.batch, tk.tp, d)
        outs.append(y[:, nm : tk.t].astype(xin.dtype))
    return tuple(outs)


def kernel(x_prompt, x_sample, meta_tokens, norm_mix_g, w_in, q_latent_g, kv_latent_g, w_uq, w_uk, w_uv, q_norm_g, k_norm_g, conv_w, conv_b, filt_w1, filt_b1, filt_freq1, filt_w2, filt_b2, filt_freq2, filt_w3, hyena_d, out_norm_g, w_out, norm_ffn_g, w_router_group, b_router_group, w_router_expert, b_router_expert, w_gate, w_up, w_down):
    p = dict(meta_tokens=meta_tokens, norm_mix_g=norm_mix_g, w_in=w_in, q_latent_g=q_latent_g,
             kv_latent_g=kv_latent_g, w_uq=w_uq, w_uk=w_uk, w_uv=w_uv, q_norm_g=q_norm_g, k_norm_g=k_norm_g,
             conv_w=conv_w, conv_b=conv_b, filt_w1=filt_w1, filt_b1=filt_b1, filt_freq1=filt_freq1,
             filt_w2=filt_w2, filt_b2=filt_b2, filt_freq2=filt_freq2, filt_w3=filt_w3, hyena_d=hyena_d,
             out_norm_g=out_norm_g, w_out=w_out, norm_ffn_g=norm_ffn_g, w_router_group=w_router_group,
             b_router_group=b_router_group, w_router_expert=w_router_expert, b_router_expert=b_router_expert,
             w_gate=w_gate, w_up=w_up, w_down=w_down)
    return encoder((x_prompt, x_sample), p, Dims())
```

```python
import dataclasses
import functools
import math

import jax
import jax.numpy as jnp
import numpy as np
from jax import lax
from jax.experimental import pallas as pl
from jax.experimental.pallas import tpu as pltpu

F32 = jnp.float32
BF16 = jnp.bfloat16
EPS = 1e-6
NEG_BIG = -1e30

LANE = 128
SUBLANE = 8
V7X_VMEM_BYTES = 64 * 1024 * 1024
VMEM_LIMIT = V7X_VMEM_BYTES - 8 * 1024 * 1024

ROW_TILE = 256
TIME_BLOCK = LANE


@dataclasses.dataclass(frozen=True)
class Dims:
    d_model: int = 4096
    n_meta: int = 16
    n_heads: int = 16
    nope: int = 128
    rope: int = 64
    v_dim: int = 128
    q_lora: int = 1024
    kv_lora: int = 512
    hy_width: int = 2048
    hy_order: int = 2
    out_group: int = 128
    filt_emb: int = 33
    filt_hidden: int = 64
    n_groups: int = 8
    epg: int = 8
    top_k: int = 2
    d_expert: int = 1024
    rope_theta: float = 10000.0
    fast_decay: float = 0.3
    slow_decay: float = 1.5
    decay_target: float = 1e-2
    moe_rows: int = 512
    moe_up_cols: int = 512
    moe_down_cols: int = 2048
    mm_rows: int = 512
    mm_cols: int = 1024
    dft_cols: int = 512

    @property
    def qk_dim(self):
        return self.nope + self.rope

    @property
    def head_pad(self):
        return _round_up(self.qk_dim, LANE)

    @property
    def n_experts(self):
        return self.n_groups * self.epg


@dataclasses.dataclass(frozen=True)
class Trunk:
    batch: int
    t: int
    tp: int
    row0: int

    @property
    def rows(self):
        return self.batch * self.tp

    @property
    def kh(self):
        return self.tp // TIME_BLOCK

    @property
    def n1(self):
        return _round_up(-(-(2 * self.t - 1) // TIME_BLOCK), SUBLANE)

    @property
    def n(self):
        return self.n1 * TIME_BLOCK


def _round_up(x, m):
    return -(-x // m) * m


def _cparams(sem):
    return pltpu.CompilerParams(dimension_semantics=sem, vmem_limit_bytes=VMEM_LIMIT)


def _dot(a, b):
    return jnp.dot(a, b, preferred_element_type=F32)


def _split_bf16(x):
    hi = x.astype(BF16)
    lo = (x - hi.astype(F32)).astype(BF16)
    return hi, lo


def _dot3(x, w_hi, w_lo):
    x_hi, x_lo = _split_bf16(x)
    return _dot(x_hi, w_hi) + _dot(x_hi, w_lo) + _dot(x_lo, w_hi)


def _seq_valid(rows, segments):
    ok = None
    for start, length in segments:
        m = (rows >= start) & (rows < start + length)
        ok = m if ok is None else (ok | m)
    return ok


def _rmsnorm_kernel(x_ref, g_ref, o_ref):
    x = x_ref[...]
    ms = jnp.mean(x * x, axis=-1, keepdims=True)
    o_ref[...] = (x * lax.rsqrt(ms + EPS) * g_ref[...]).astype(o_ref.dtype)


def rmsnorm(x, g, tm=ROW_TILE):
    r, d = x.shape
    return pl.pallas_call(
        _rmsnorm_kernel,
        out_shape=jax.ShapeDtypeStruct((r, d), BF16),
        grid=(r // tm,),
        in_specs=[pl.BlockSpec((tm, d), lambda i: (i, 0)), pl.BlockSpec((1, d), lambda i: (0, 0))],
        out_specs=pl.BlockSpec((tm, d), lambda i: (i, 0)),
        compiler_params=_cparams(("parallel",)),
        name="rmsnorm",
    )(x, g.reshape(1, d))


def _rmsnorm_router_kernel(x_ref, g_ref, whi_ref, wlo_ref, h_ref, lg_ref):
    x = x_ref[...]
    ms = jnp.mean(x * x, axis=-1, keepdims=True)
    h = x * lax.rsqrt(ms + EPS) * g_ref[...]
    h_ref[...] = h.astype(h_ref.dtype)
    lg_ref[...] = _dot3(h, whi_ref[...], wlo_ref[...])


def rmsnorm_router(x, g, w_router, tm=ROW_TILE):
    r, d = x.shape
    nc = _round_up(w_router.shape[1], LANE)
    w = jnp.pad(w_router.astype(F32), ((0, 0), (0, nc - w_router.shape[1])))
    w_hi, w_lo = _split_bf16(w)
    return pl.pallas_call(
        _rmsnorm_router_kernel,
        out_shape=(jax.ShapeDtypeStruct((r, d), F32), jax.ShapeDtypeStruct((r, nc), F32)),
        grid=(r // tm,),
        in_specs=[
            pl.BlockSpec((tm, d), lambda i: (i, 0)),
            pl.BlockSpec((1, d), lambda i: (0, 0)),
            pl.BlockSpec((d, nc), lambda i: (0, 0)),
            pl.BlockSpec((d, nc), lambda i: (0, 0)),
        ],
        out_specs=(pl.BlockSpec((tm, d), lambda i: (i, 0)), pl.BlockSpec((tm, nc), lambda i: (i, 0))),
        compiler_params=_cparams(("parallel",)),
        name="rmsnorm_router",
    )(x, g.reshape(1, d), w_hi, w_lo)


def _mm_kernel(a_ref, b_ref, o_ref):
    o_ref[...] = _dot(a_ref[...], b_ref[...]).astype(o_ref.dtype)


def matmul(a, b, out_dtype, tm, tn):
    m, k = a.shape
    n = b.shape[1]
    return pl.pallas_call(
        _mm_kernel,
        out_shape=jax.ShapeDtypeStruct((m, n), out_dtype),
        grid=(m // tm, n // tn),
        in_specs=[pl.BlockSpec((tm, k), lambda i, j: (i, 0)), pl.BlockSpec((k, tn), lambda i, j: (0, j))],
        out_specs=pl.BlockSpec((tm, tn), lambda i, j: (i, j)),
        compiler_params=_cparams(("parallel", "parallel")),
        name="matmul",
    )(a, b)


def _mm_residual_kernel(a_ref, b_ref, r_ref, o_ref, *, tm, segments):
    acc = _dot(a_ref[...], b_ref[...])
    rows = pl.program_id(0) * tm + lax.broadcasted_iota(jnp.int32, (tm, 1), 0)
    o_ref[...] = jnp.where(_seq_valid(rows, segments), r_ref[...] + acc, 0.0)


def matmul_residual(a, b, res, segments, tm, tn):
    m, k = a.shape
    n = b.shape[1]
    return pl.pallas_call(
        functools.partial(_mm_residual_kernel, tm=tm, segments=segments),
        out_shape=jax.ShapeDtypeStruct((m, n), F32),
        grid=(m // tm, n // tn),
        in_specs=[
            pl.BlockSpec((tm, k), lambda i, j: (i, 0)),
            pl.BlockSpec((k, tn), lambda i, j: (0, j)),
            pl.BlockSpec((tm, tn), lambda i, j: (i, j)),
        ],
        out_specs=pl.BlockSpec((tm, tn), lambda i, j: (i, j)),
        compiler_params=_cparams(("parallel", "parallel")),
        name="matmul_residual",
    )(a, b, res)


def _latent_kernel(h_ref, w_ref, gq_ref, gkv_ref, cq_ref, ckv_ref, kpe_ref, *, q_lora, kv_lora):
    p = _dot(h_ref[...], w_ref[...])
    q = p[:, :q_lora]
    kv = p[:, q_lora : q_lora + kv_lora]
    cq_ref[...] = (q * lax.rsqrt(jnp.mean(q * q, axis=-1, keepdims=True) + EPS) * gq_ref[...]).astype(cq_ref.dtype)
    ckv_ref[...] = (kv * lax.rsqrt(jnp.mean(kv * kv, axis=-1, keepdims=True) + EPS) * gkv_ref[...]).astype(
        ckv_ref.dtype
    )
    kpe_ref[...] = p[:, q_lora + kv_lora :]


def latent_proj(h, w_lat, gq, gkv, dm, tm=ROW_TILE):
    r, d = h.shape
    nl = w_lat.shape[1]
    return pl.pallas_call(
        functools.partial(_latent_kernel, q_lora=dm.q_lora, kv_lora=dm.kv_lora),
        out_shape=(
            jax.ShapeDtypeStruct((r, dm.q_lora), BF16),
            jax.ShapeDtypeStruct((r, dm.kv_lora), BF16),
            jax.ShapeDtypeStruct((r, LANE), F32),
        ),
        grid=(r // tm,),
        in_specs=[
            pl.BlockSpec((tm, d), lambda i: (i, 0)),
            pl.BlockSpec((d, nl), lambda i: (0, 0)),
            pl.BlockSpec((1, dm.q_lora), lambda i: (0, 0)),
            pl.BlockSpec((1, dm.kv_lora), lambda i: (0, 0)),
        ],
        out_specs=(
            pl.BlockSpec((tm, dm.q_lora), lambda i: (i, 0)),
            pl.BlockSpec((tm, dm.kv_lora), lambda i: (i, 0)),
            pl.BlockSpec((tm, LANE), lambda i: (i, 0)),
        ),
        compiler_params=_cparams(("parallel",)),
        name="latent_proj",
    )(h, w_lat, gq.reshape(1, -1), gkv.reshape(1, -1))


def _rope_tail(c, cos_t, sin_a, sin_b, half):
    return c * cos_t + pltpu.roll(c, LANE - half, 1) * sin_a + pltpu.roll(c, half, 1) * sin_b


def _qkv_kernel(cq_ref, ckv_ref, kpe_ref, cos_ref, sa_ref, sb_ref, wq_ref, wk_ref, wv_ref, gq_ref, gk_ref,
                q_ref, k_ref, v_ref, *, dm):
    hp, nope, qk = dm.head_pad, dm.nope, dm.qk_dim
    half = dm.rope // 2
    scale = qk ** -0.5
    ckv = ckv_ref[...]
    q = _dot(cq_ref[...], wq_ref[...])
    kn = _dot(ckv, wk_ref[...])
    v_ref[...] = _dot(ckv, wv_ref[...]).astype(v_ref.dtype)
    pe = kpe_ref[...]
    pe_ss = jnp.sum(pe * pe, axis=-1, keepdims=True)
    cos_t, sin_a, sin_b = cos_ref[...], sa_ref[...], sb_ref[...]
    gq = gq_ref[...]
    gk = gk_ref[...]
    for h in range(dm.n_heads):
        qh = q[:, h * hp : (h + 1) * hp]
        r = lax.rsqrt(jnp.sum(qh * qh, axis=-1, keepdims=True) * (1.0 / qk) + EPS) * scale
        qa = qh[:, :nope] * r * gq[:, :nope]
        qc = _rope_tail(qh[:, nope:] * r * gq[:, nope:], cos_t, sin_a, sin_b, half)
        q_ref[:, h * hp : h * hp + nope] = qa.astype(q_ref.dtype)
        q_ref[:, h * hp + nope : (h + 1) * hp] = qc.astype(q_ref.dtype)
        kh = kn[:, h * nope : (h + 1) * nope]
        rk = lax.rsqrt((jnp.sum(kh * kh, axis=-1, keepdims=True) + pe_ss) * (1.0 / qk) + EPS)
        ka = kh * rk * gk[:, :nope]
        kc = _rope_tail(pe * rk * gk[:, nope:], cos_t, sin_a, sin_b, half)
        k_ref[:, h * hp : h * hp + nope] = ka.astype(k_ref.dtype)
        k_ref[:, h * hp + nope : (h + 1) * hp] = kc.astype(k_ref.dtype)


def qkv_heads(cq, ckv, kpe, rope_tabs, wq, wk, wv, gq, gk, dm, tk, tm=ROW_TILE):
    off = tk.row0 // tm
    hp, nh = dm.head_pad, dm.n_heads
    row = lambda i: (off + i, 0)
    loc = lambda i: (i, 0)
    fixed = lambda i: (0, 0)
    return pl.pallas_call(
        functools.partial(_qkv_kernel, dm=dm),
        out_shape=(
            jax.ShapeDtypeStruct((tk.rows, nh * hp), BF16),
            jax.ShapeDtypeStruct((tk.rows, nh * hp), BF16),
            jax.ShapeDtypeStruct((tk.rows, nh * dm.v_dim), BF16),
        ),
        grid=(tk.rows // tm,),
        in_specs=[
            pl.BlockSpec((tm, dm.q_lora), row),
            pl.BlockSpec((tm, dm.kv_lora), row),
            pl.BlockSpec((tm, LANE), row),
            pl.BlockSpec((tm, LANE), loc),
            pl.BlockSpec((tm, LANE), loc),
            pl.BlockSpec((tm, LANE), loc),
            pl.BlockSpec(wq.shape, fixed),
            pl.BlockSpec(wk.shape, fixed),
            pl.BlockSpec(wv.shape, fixed),
            pl.BlockSpec((1, hp), fixed),
            pl.BlockSpec((1, hp), fixed),
        ],
        out_specs=(
            pl.BlockSpec((tm, nh * hp), loc),
            pl.BlockSpec((tm, nh * hp), loc),
            pl.BlockSpec((tm, nh * dm.v_dim), loc),
        ),
        compiler_params=_cparams(("parallel",)),
        name="qkv_heads",
    )(cq, ckv, kpe, *rope_tabs, wq, wk, wv, gq, gk)


def _attn_kernel(q_ref, k_ref, v_ref, o_ref, *, t, tk, tk_last, unroll):
    q = q_ref[...]
    tq = q.shape[0]
    n_full, rem = divmod(t, tk)

    def chunk(start, size, carry, valid):
        m, l, acc = carry
        k = k_ref[pl.ds(start, size), :]
        v = v_ref[pl.ds(start, size), :]
        s = lax.dot_general(q, k, (((1,), (1,)), ((), ())), preferred_element_type=F32)
        if valid < size:
            col = lax.broadcasted_iota(jnp.int32, s.shape, 1)
            s = jnp.where(col < valid, s, NEG_BIG)
        m_new = jnp.maximum(m, jnp.max(s, axis=-1, keepdims=True))
        alpha = jnp.exp(m - m_new)
        p = jnp.exp(s - m_new)
        l = alpha * l + jnp.sum(p, axis=-1, keepdims=True)
        acc = alpha * acc + _dot(p.astype(v.dtype), v)
        return m_new, l, acc

    carry = (jnp.full((tq, 1), NEG_BIG, F32), jnp.zeros((tq, 1), F32), jnp.zeros((tq, v_ref.shape[1]), F32))
    carry = lax.fori_loop(
        0, n_full, lambda c, cr: chunk(pl.multiple_of(c * tk, tk), tk, cr, tk), carry, unroll=unroll)
    if rem:
        carry = chunk(n_full * tk, tk_last, carry, rem)
    _, l, acc = carry
    o_ref[...] = (acc / l).astype(o_ref.dtype)


def _attn_tiles(tk, max_rows, tkc):
    nq = next(n for n in range(1, tk.tp + 1)
              if tk.tp % n == 0 and (tk.tp // n) % (2 * SUBLANE) == 0 and tk.tp // n <= max_rows)
    n_full, rem = divmod(tk.t, tkc)
    tk_last = min(tkc, _round_up(rem, ROW_TILE)) if rem else 0
    assert n_full * tkc + tk_last <= tk.tp
    return nq, tk_last


def attention(q, k, v, dm, tk, max_rows=1152, tkc=512, unroll=2):
    nq, tk_last = _attn_tiles(tk, max_rows, tkc)
    tq = tk.tp // nq
    hp = dm.head_pad
    return pl.pallas_call(
        functools.partial(_attn_kernel, t=tk.t, tk=tkc, tk_last=tk_last, unroll=unroll),
        out_shape=jax.ShapeDtypeStruct((tk.rows, dm.n_heads * dm.v_dim), F32),
        grid=(tk.batch, dm.n_heads, nq),
        in_specs=[
            pl.BlockSpec((tq, hp), lambda b, h, i: (b * nq + i, h)),
            pl.BlockSpec((tk.tp, hp), lambda b, h, i: (b, h)),
            pl.BlockSpec((tk.tp, dm.v_dim), lambda b, h, i: (b, h)),
        ],
        out_specs=pl.BlockSpec((tq, dm.v_dim), lambda b, h, i: (b * nq + i, h)),
        compiler_params=_cparams(("parallel", "parallel", "arbitrary")),
        name="attention",
    )(q, k, v)


def _group_norm_kernel(a_ref, hy_ref, g_ref, o_ref, *, group):
    wa = a_ref.shape[1]
    g = g_ref[...]
    for src, base in ((a_ref, 0), (hy_ref, wa)):
        for c in range(src.shape[1] // group):
            xc = src[:, c * group : (c + 1) * group]
            y = xc * lax.rsqrt(jnp.mean(xc * xc, axis=-1, keepdims=True) + EPS)
            lo = base + c * group
            o_ref[:, lo : lo + group] = (y * g[:, lo : lo + group]).astype(o_ref.dtype)


def group_norm(a, hy, g, group, tm=ROW_TILE):
    r, wa = a.shape
    wh = hy.shape[1]
    return pl.pallas_call(
        functools.partial(_group_norm_kernel, group=group),
        out_shape=jax.ShapeDtypeStruct((r, wa + wh), BF16),
        grid=(r // tm,),
        in_specs=[
            pl.BlockSpec((tm, wa), lambda i: (i, 0)),
            pl.BlockSpec((tm, wh), lambda i: (i, 0)),
            pl.BlockSpec((1, wa + wh), lambda i: (0, 0)),
        ],
        out_specs=pl.BlockSpec((tm, wa + wh), lambda i: (i, 0)),
        compiler_params=_cparams(("parallel",)),
        name="group_norm",
    )(a, hy, g.reshape(1, -1))


def _conv3_kernel(u_ref, prev_ref, next_ref, w_ref, b_ref, o_ref, *, tm, t, tp, starts_at_row0):
    i = pl.program_id(0)
    u = u_ref[...]
    r = lax.broadcasted_iota(jnp.int32, (tm, 1), 0)
    prev_row = prev_ref[SUBLANE - 1 : SUBLANE, :]
    if starts_at_row0:
        prev_row = jnp.where(i == 0, 0.0, prev_row)
    up = jnp.where(r == 0, prev_row, pltpu.roll(u, 1, 0))
    dn = jnp.where(r == tm - 1, next_ref[0:1, :], pltpu.roll(u, tm - 1, 0))
    w = w_ref[...]
    uc = up * w[0:1, :] + u * w[1:2, :] + dn * w[2:3, :] + b_ref[...]
    pos = (i % (tp // tm)) * tm + r
    o_ref[...] = jnp.where(pos < t, uc, 0.0)


def conv3(u, w, b, tk, n_split, tm=ROW_TILE):
    cols = u.shape[1]
    tc = cols // n_split
    off = tk.row0 // tm
    sub = tm // SUBLANE
    last = u.shape[0] // SUBLANE - 1
    return pl.pallas_call(
        functools.partial(_conv3_kernel, tm=tm, t=tk.t, tp=tk.tp, starts_at_row0=(tk.row0 == 0)),
        out_shape=jax.ShapeDtypeStruct((n_split, tk.rows, tc), F32),
        grid=(tk.rows // tm, n_split),
        in_specs=[
            pl.BlockSpec((tm, tc), lambda i, j: (off + i, j)),
            pl.BlockSpec((SUBLANE, tc), lambda i, j: (jnp.maximum((off + i) * sub - 1, 0), j)),
            pl.BlockSpec((SUBLANE, tc), lambda i, j: (jnp.minimum((off + i + 1) * sub, last), j)),
            pl.BlockSpec((3, tc), lambda i, j: (0, j)),
            pl.BlockSpec((1, tc), lambda i, j: (0, j)),
        ],
        out_specs=pl.BlockSpec((None, tm, tc), lambda i, j: (j, i, 0)),
        compiler_params=_cparams(("parallel", "parallel")),
        name="conv3",
    )(u, u, u, w, b.reshape(1, -1))


def _taps_kernel(fvec_ref, w1h_ref, w1l_ref, b1_ref, f1_ref, w2h_ref, w2l_ref, b2_ref, f2_ref, w3h_ref, w3l_ref,
                 dl_ref, o_ref, *, tr, t, n, n_orders, bands):
    p = pl.program_id(0) * tr + lax.broadcasted_iota(jnp.int32, (tr, 1), 0)
    fwd = p < t
    bwd = p > n - t
    lag = jnp.where(fwd | bwd, jnp.where(fwd, p, n - p), 0).astype(F32)
    t_norm = lag * (1.0 / (t - 1))
    arg = ((2.0 * math.pi) * lag / t) * fvec_ref[...]
    lane = lax.broadcasted_iota(jnp.int32, (1, LANE), 1)
    cos_l = (lane >= 1) & (lane <= bands)
    sin_l = (lane > bands) & (lane <= 2 * bands)
    feats = jnp.where(lane == 0, t_norm, jnp.where(cos_l, jnp.cos(arg), jnp.where(sin_l, -jnp.sin(arg), 0.0)))
    hid = jnp.sin(f1_ref[...] * (_dot3(feats, w1h_ref[...], w1l_ref[...]) + b1_ref[...]))
    hid = jnp.sin(f2_ref[...] * (_dot3(hid, w2h_ref[...], w2l_ref[...]) + b2_ref[...]))
    decay = jnp.exp(-t_norm * dl_ref[...])
    for o in range(n_orders):
        f_fwd = _dot3(hid, w3h_ref[2 * o], w3l_ref[2 * o])
        f_bwd = _dot3(hid, w3h_ref[2 * o + 1], w3l_ref[2 * o + 1])
        o_ref[o] = jnp.where(fwd, f_fwd, jnp.where(bwd, f_bwd, 0.0)) * decay


def hyena_taps(fp, tk, dm, tr=ROW_TILE, tc=2048):
    n = tk.n
    c = dm.hy_width
    tc = min(tc, c)
    no = dm.hy_order
    fixed2 = lambda i, j: (0, 0)
    sq = pl.BlockSpec((LANE, LANE), fixed2)
    vec = pl.BlockSpec((1, LANE), fixed2)
    return pl.pallas_call(
        functools.partial(_taps_kernel, tr=tr, t=tk.t, n=n, n_orders=no, bands=(dm.filt_emb - 1) // 2),
        out_shape=jax.ShapeDtypeStruct((no, n, c), F32),
        grid=(n // tr, c // tc),
        in_specs=[vec, sq, sq, vec, vec, sq, sq, vec, vec,
                  pl.BlockSpec((2 * no, LANE, tc), lambda i, j: (0, 0, j)),
                  pl.BlockSpec((2 * no, LANE, tc), lambda i, j: (0, 0, j)),
                  pl.BlockSpec((1, tc), lambda i, j: (0, j))],
        out_specs=pl.BlockSpec((no, tr, tc), lambda i, j: (0, i, j)),
        compiler_params=_cparams(("parallel", "parallel")),
        name="hyena_taps",
    )(fp["fvec"], fp["w1h"], fp["w1l"], fp["b1"], fp["f1"], fp["w2h"], fp["w2l"], fp["b2"], fp["f2"], fp["w3h"],
      fp["w3l"], fp["deltas"])


def _pdft_fwd_kernel(g_ref, z_ref, ar_ref, ai_ref, xs_ref, *, parts, rows, n1):
    xs_ref[...] = jnp.zeros_like(xs_ref)
    half = xs_ref.shape[0] // parts
    for r in range(SUBLANE):
        for p in range(parts):
            xs_ref[p * half : p * half + rows, :] = z_ref[p, :, r, :]
        res = _dot(g_ref[r], xs_ref[...].astype(BF16))
        ar_ref[:, r, :] = res[:n1]
        ai_ref[:, r, :] = res[n1:]


def pdft_fwd(g, z, n1, which, dm):
    s0, sn = which
    _, parts, rows, _, c = z.shape
    cb = min(dm.dft_cols, c)
    kpad = g.shape[2]
    out = jax.ShapeDtypeStruct((sn, n1, TIME_BLOCK, c), F32)
    oblk = pl.BlockSpec((None, n1, SUBLANE, cb), lambda s, j, k: (s, 0, j, k))
    return pl.pallas_call(
        functools.partial(_pdft_fwd_kernel, parts=parts, rows=rows, n1=n1),
        out_shape=(out, out),
        grid=(sn, TIME_BLOCK // SUBLANE, c // cb),
        in_specs=[
            pl.BlockSpec((SUBLANE, 2 * n1, kpad), lambda s, j, k: (j, 0, 0)),
            pl.BlockSpec((None, parts, rows, SUBLANE, cb), lambda s, j, k: (s0 + s, 0, 0, j, k)),
        ],
        out_specs=(oblk, oblk),
        scratch_shapes=[pltpu.VMEM((kpad, cb), F32)],
        compiler_params=_cparams(("parallel", "parallel", "parallel")),
        name="pdft_fwd",
    )(g, z)


def _spectrum_kernel(ar_ref, ai_ref, f_ref, h_ref):
    f = f_ref[...]
    for k in range(ar_ref.shape[0]):
        a = jnp.concatenate([ar_ref[k], ai_ref[k]], axis=0).astype(BF16)
        h_ref[k] = _dot(f, a)


def filter_spectrum(ar, ai, f2f, dm, kb=SUBLANE):
    no, n1, _, c = ar.shape
    cb = min(dm.dft_cols, c)
    blk = pl.BlockSpec((None, kb, TIME_BLOCK, cb), lambda o, i, j: (o, i, 0, j))
    return pl.pallas_call(
        _spectrum_kernel,
        out_shape=jax.ShapeDtypeStruct((no, n1, 2 * TIME_BLOCK, c), F32),
        grid=(no, n1 // kb, c // cb),
        in_specs=[blk, blk, pl.BlockSpec(f2f.shape, lambda o, i, j: (0, 0))],
        out_specs=pl.BlockSpec((None, kb, 2 * TIME_BLOCK, cb), lambda o, i, j: (o, i, 0, j)),
        compiler_params=_cparams(("parallel", "parallel", "parallel")),
        name="filter_spectrum",
    )(ar, ai, f2f)


def _freq_mul_kernel(ar_ref, ai_ref, h_ref, ff_ref, fi_ref, br_ref, bi_ref):
    ff = ff_ref[...]
    fi = fi_ref[...]
    nb = TIME_BLOCK
    for k in range(ar_ref.shape[0]):
        x = _dot(ff, jnp.concatenate([ar_ref[k], ai_ref[k]], axis=0).astype(BF16))
        h = h_ref[k]
        xr, xi = x[:nb], x[nb:]
        hr, hi = h[:nb], h[nb:]
        y = jnp.concatenate([xr * hr - xi * hi, xr * hi + xi * hr], axis=0).astype(BF16)
        b = _dot(fi, y)
        br_ref[k] = b[:nb]
        bi_ref[k] = b[nb:]


def freq_multiply(ar, ai, h, order, f2f, f2i, dm, kb=SUBLANE):
    _, n1, _, c = ar.shape
    cb = min(dm.dft_cols, c)
    ablk = pl.BlockSpec((None, kb, TIME_BLOCK, cb), lambda i, j: (0, i, 0, j))
    blk = pl.BlockSpec((kb, TIME_BLOCK, cb), lambda i, j: (i, 0, j))
    out = jax.ShapeDtypeStruct((n1, TIME_BLOCK, c), F32)
    return pl.pallas_call(
        _freq_mul_kernel,
        out_shape=(out, out),
        grid=(n1 // kb, c // cb),
        in_specs=[ablk, ablk,
                  pl.BlockSpec((None, kb, 2 * TIME_BLOCK, cb), lambda i, j: (order, i, 0, j)),
                  pl.BlockSpec(f2f.shape, lambda i, j: (0, 0)),
                  pl.BlockSpec(f2i.shape, lambda i, j: (0, 0))],
        out_specs=(blk, blk),
        compiler_params=_cparams(("parallel", "parallel")),
        name="freq_multiply",
    )(ar, ai, h, f2f, f2i)


def _pdft_inv_gate_kernel(g_ref, br_ref, bi_ref, gate_ref, z_ref, d_ref, o_ref, bs_ref, *, n1, rows):
    bs_ref[...] = jnp.zeros_like(bs_ref)
    half = bs_ref.shape[0] // 2
    mh = g_ref.shape[1] // 2
    d = d_ref[...]
    for r in range(SUBLANE):
        bs_ref[0:n1, :] = br_ref[:, r, :]
        bs_ref[half : half + n1, :] = bi_ref[:, r, :]
        y = _dot(g_ref[r], bs_ref[...].astype(BF16))
        for b in range(2):
            o_ref[b, :, r, :] = gate_ref[b, :, r, :] * (y[b * mh : b * mh + rows] + d * z_ref[b, :, r, :])


def pdft_inv_gate(ginv, br, bi, sig, gate_idx, z, z_idx, d, tk, dm):
    n1, _, c = br.shape
    cb = min(dm.dft_cols, c)
    kh = tk.kh
    kpad = ginv.shape[2]
    seq = lambda idx: pl.BlockSpec((None, 2, kh, SUBLANE, cb), lambda j, k: (idx, 0, 0, j, k))
    bblk = pl.BlockSpec((n1, SUBLANE, cb), lambda j, k: (0, j, k))
    return pl.pallas_call(
        functools.partial(_pdft_inv_gate_kernel, n1=n1, rows=kh),
        out_shape=jax.ShapeDtypeStruct((1, 2, kh, TIME_BLOCK, c), F32),
        grid=(TIME_BLOCK // SUBLANE, c // cb),
        in_specs=[
            pl.BlockSpec((SUBLANE, ginv.shape[1], kpad), lambda j, k: (j, 0, 0)),
            bblk,
            bblk,
            seq(gate_idx),
            seq(z_idx),
            pl.BlockSpec((1, cb), lambda j, k: (0, k)),
        ],
        out_specs=seq(0),
        scratch_shapes=[pltpu.VMEM((kpad, cb), F32)],
        compiler_params=_cparams(("parallel", "parallel")),
        name="pdft_inv_gate",
    )(ginv, br, bi, sig, z, d.reshape(1, c))


def _dft_real_form(phase_num, n, sign, scale=1.0):
    ang = (sign * 2.0 * math.pi / n) * (phase_num % n).astype(F32)
    c = jnp.cos(ang) * scale
    s = jnp.sin(ang) * scale
    top = jnp.concatenate([c, -s], axis=-1)
    bot = jnp.concatenate([s, c], axis=-1)
    return jnp.concatenate([top, bot], axis=-2).astype(BF16)


def dft_tables(tk):
    n1, n, nb = tk.n1, tk.n, TIME_BLOCK
    i32 = jnp.int32
    n2 = jnp.arange(nb, dtype=i32)[:, None, None]
    k1 = jnp.arange(n1, dtype=i32)[None, :, None]

    def major(cols):
        col = jnp.arange(cols, dtype=i32)[None, None, :]
        return k1 * (nb * col + n2)

    sig_half = _round_up(tk.kh, 64)
    tap_half = _round_up(n1, LANE)
    out_half = _round_up(tk.kh, SUBLANE)
    g_sig = _dft_real_form(major(sig_half), n, -1.0)
    g_tap = _dft_real_form(major(tap_half), n, -1.0)[:, :, :tap_half]
    rows = jnp.arange(out_half, dtype=i32)[None, :, None]
    kcol = jnp.arange(tap_half, dtype=i32)[None, None, :]
    g_inv = _dft_real_form(kcol * (nb * rows + n2), n, 1.0, 1.0 / n)
    kk = jnp.arange(nb, dtype=i32)
    minor = kk[:, None] * kk[None, :]
    f2f = _dft_real_form(minor, nb, -1.0)
    f2i = _dft_real_form(minor, nb, 1.0)
    return dict(g_sig=g_sig, g_tap=g_tap, g_inv=g_inv, f2f=f2f, f2i=f2i)


def filter_params(p, l, dm):
    emb, hid, c, no = dm.filt_emb, dm.filt_hidden, dm.hy_width, dm.hy_order
    bands = (emb - 1) // 2

    def pad2(w, r, cc):
        return jnp.pad(w.astype(F32), ((0, r - w.shape[0]), (0, cc - w.shape[1])))

    def padv(v):
        return jnp.pad(v.astype(F32), (0, LANE - v.shape[0])).reshape(1, LANE)

    freqs = jnp.linspace(1e-4, bands - 1, bands, dtype=F32)
    fvec = jnp.concatenate([jnp.zeros((1,), F32), freqs, freqs, jnp.zeros((LANE - emb,), F32)]).reshape(1, LANE)
    w1h, w1l = _split_bf16(pad2(p["filt_w1"][l], LANE, LANE))
    w2h, w2l = _split_bf16(pad2(p["filt_w2"][l], LANE, LANE))
    w3 = p["filt_w3"][l].astype(F32).reshape(hid, 2 * no, c).transpose(1, 0, 2)
    w3h, w3l = _split_bf16(jnp.pad(w3, ((0, 0), (0, LANE - hid), (0, 0))))
    deltas = jnp.abs(jnp.linspace(math.log(dm.decay_target) / dm.slow_decay,
                                  math.log(dm.decay_target) / dm.fast_decay, c, dtype=F32)).reshape(1, c)
    return dict(fvec=fvec, w1h=w1h, w1l=w1l, b1=padv(p["filt_b1"][l]), f1=padv(p["filt_freq1"][l]),
                w2h=w2h, w2l=w2l, b2=padv(p["filt_b2"][l]), f2=padv(p["filt_freq2"][l]),
                w3h=w3h, w3l=w3l, deltas=deltas)


def hyena(u, p, l, tk, tabs, dm):
    c, no = dm.hy_width, dm.hy_order
    n1, kh = tk.n1, tk.kh
    assert tk.batch == 2, "the two sequences of a trunk are the real and imaginary parts of one transform"
    sig = conv3(u, p["conv_w"][l], p["conv_b"][l], tk, no + 1)
    sig = sig.reshape(no + 1, tk.batch, kh, TIME_BLOCK, c)
    taps = hyena_taps(filter_params(p, l, dm), tk, dm)
    tr, ti = pdft_fwd(tabs["g_tap"], taps.reshape(no, 1, n1, TIME_BLOCK, c), n1, (0, no), dm)
    spec = filter_spectrum(tr, ti, tabs["f2f"], dm)
    z = sig
    for o in range(no):
        ar, ai = pdft_fwd(tabs["g_sig"], z, n1, (0, 1), dm)
        br, bi = freq_multiply(ar, ai, spec, o, tabs["f2f"], tabs["f2i"], dm)
        z = pdft_inv_gate(tabs["g_inv"], br, bi, sig, o + 1, z, 0, p["hyena_d"][l, o], tk, dm)
    return z.reshape(tk.rows, c)


def _row_copy(src_ref, src_row, dst_ref, dst_row, sem):
    return pltpu.make_async_copy(src_ref.at[pl.ds(src_row, 1)], dst_ref.at[pl.ds(dst_row, 1)], sem)


DMA_ISSUE_UNROLL = 8


def _gather_rows_kernel(nrows_ref, idx_ref, src_ref, o_ref, buf_ref, sem, *, tg):
    live = pl.program_id(0) * tg < nrows_ref[0]

    @pl.when(live)
    def _():
        def issue(i, _):
            _row_copy(src_ref, idx_ref[i], buf_ref, i, sem).start()
            return 0

        lax.fori_loop(0, tg, issue, 0, unroll=DMA_ISSUE_UNROLL)

        def drain(i, _):
            _row_copy(src_ref, 0, buf_ref, i, sem).wait()
            return 0

        lax.fori_loop(0, tg, drain, 0, unroll=DMA_ISSUE_UNROLL)
        o_ref[...] = buf_ref[...].astype(o_ref.dtype)

    @pl.when(jnp.logical_not(live))
    def _():
        o_ref[...] = jnp.zeros_like(o_ref)


def gather_rows(src, idx, n_live, out_dtype, tg=ROW_TILE):
    n = idx.shape[0]
    d = src.shape[1]
    return pl.pallas_call(
        functools.partial(_gather_rows_kernel, tg=tg),
        out_shape=jax.ShapeDtypeStruct((n, d), out_dtype),
        grid_spec=pltpu.PrefetchScalarGridSpec(
            num_scalar_prefetch=1,
            grid=(n // tg,),
            in_specs=[
                pl.BlockSpec((tg,), lambda i, nr: (i,), memory_space=pltpu.SMEM),
                pl.BlockSpec(memory_space=pl.ANY),
            ],
            out_specs=pl.BlockSpec((tg, d), lambda i, nr: (i, 0)),
            scratch_shapes=[pltpu.VMEM((tg, d), src.dtype), pltpu.SemaphoreType.DMA],
        ),
        compiler_params=_cparams(("arbitrary",)),
        name="moe_gather",
    )(n_live, idx, src)


def _expert_up_kernel(be_ref, first_ref, nb_ref, x_ref, wg_ref, wu_ref, o_ref, wgs_ref, wus_ref):
    m = pl.program_id(1)

    @pl.when(m < nb_ref[0])
    def _():
        @pl.when(first_ref[m] == 1)
        def _():
            wgs_ref[...] = wg_ref[...].astype(BF16)
            wus_ref[...] = wu_ref[...].astype(BF16)

        x = x_ref[...]
        g = _dot(x, wgs_ref[...])
        u = _dot(x, wus_ref[...])
        o_ref[...] = (g * jax.nn.sigmoid(g) * u).astype(o_ref.dtype)

    @pl.when(m >= nb_ref[0])
    def _():
        o_ref[...] = jnp.zeros_like(o_ref)


def expert_up(xs, w_gate, w_up, l, blk_expert, blk_first, n_blocks, dm):
    cap, d = xs.shape
    tm, tf = dm.moe_rows, min(dm.moe_up_cols, dm.d_expert)
    nblk = cap // tm
    wspec = pl.BlockSpec((None, None, d, tf), lambda j, m, be, fi, nb: (l, be[m], 0, j))
    return pl.pallas_call(
        _expert_up_kernel,
        out_shape=jax.ShapeDtypeStruct((cap, dm.d_expert), BF16),
        grid_spec=pltpu.PrefetchScalarGridSpec(
            num_scalar_prefetch=3,
            grid=(dm.d_expert // tf, nblk),
            in_specs=[pl.BlockSpec((tm, d), lambda j, m, be, fi, nb: (m, 0)), wspec, wspec],
            out_specs=pl.BlockSpec((tm, tf), lambda j, m, be, fi, nb: (m, j)),
            scratch_shapes=[pltpu.VMEM((d, tf), BF16), pltpu.VMEM((d, tf), BF16)],
        ),
        compiler_params=_cparams(("arbitrary", "arbitrary")),
        name="moe_expert_up",
    )(blk_expert, blk_first, n_blocks, xs, w_gate, w_up)


def _expert_down_kernel(be_ref, first_ref, nb_ref, h_ref, wd_ref, o_ref, wds_ref):
    m = pl.program_id(1)

    @pl.when(m < nb_ref[0])
    def _():
        @pl.when(first_ref[m] == 1)
        def _():
            wds_ref[...] = wd_ref[...].astype(BF16)

        o_ref[...] = _dot(h_ref[...], wds_ref[...]).astype(o_ref.dtype)

    @pl.when(m >= nb_ref[0])
    def _():
        o_ref[...] = jnp.zeros_like(o_ref)


def expert_down(hs, w_down, l, blk_expert, blk_first, n_blocks, dm):
    cap, f = hs.shape
    d = w_down.shape[3]
    tm, tn = dm.moe_rows, min(dm.moe_down_cols, d)
    nblk = cap // tm
    return pl.pallas_call(
        _expert_down_kernel,
        out_shape=jax.ShapeDtypeStruct((cap, d), F32),
        grid_spec=pltpu.PrefetchScalarGridSpec(
            num_scalar_prefetch=3,
            grid=(d // tn, nblk),
            in_specs=[pl.BlockSpec((tm, f), lambda j, m, be, fi, nb: (m, 0)),
                      pl.BlockSpec((None, None, f, tn), lambda j, m, be, fi, nb: (l, be[m], 0, j))],
            out_specs=pl.BlockSpec((tm, tn), lambda j, m, be, fi, nb: (m, j)),
            scratch_shapes=[pltpu.VMEM((f, tn), BF16)],
        ),
        compiler_params=_cparams(("arbitrary", "arbitrary")),
        name="moe_expert_down",
    )(blk_expert, blk_first, n_blocks, hs, w_down)


def _combine_kernel(pos_ref, x_ref, w_ref, ys_ref, o_ref, buf_ref, sem, *, tm, top_k):
    def issue(i, _):
        for k in range(top_k):
            pltpu.make_async_copy(ys_ref.at[pl.ds(pos_ref[i * top_k + k], 1)], buf_ref.at[k, pl.ds(i, 1)], sem).start()
        return 0

    lax.fori_loop(0, tm, issue, 0, unroll=DMA_ISSUE_UNROLL // top_k)

    def drain(i, _):
        for k in range(top_k):
            pltpu.make_async_copy(ys_ref.at[pl.ds(0, 1)], buf_ref.at[k, pl.ds(i, 1)], sem).wait()
        return 0

    lax.fori_loop(0, tm, drain, 0, unroll=DMA_ISSUE_UNROLL // top_k)
    w = w_ref[...]
    acc = x_ref[...]
    for k in range(top_k):
        acc = acc + w[:, k : k + 1] * buf_ref[k].astype(F32)
    o_ref[...] = acc


def moe_combine(x, ys, pos, wts, dm, tm=ROW_TILE):
    r, d = x.shape
    k = dm.top_k
    return pl.pallas_call(
        functools.partial(_combine_kernel, tm=tm, top_k=k),
        out_shape=jax.ShapeDtypeStruct((r, d), F32),
        grid=(r // tm,),
        in_specs=[
            pl.BlockSpec((tm * k,), lambda i: (i,), memory_space=pltpu.SMEM),
            pl.BlockSpec((tm, d), lambda i: (i, 0)),
            pl.BlockSpec((tm, k), lambda i: (i, 0)),
            pl.BlockSpec(memory_space=pl.ANY),
        ],
        out_specs=pl.BlockSpec((tm, d), lambda i: (i, 0)),
        scratch_shapes=[pltpu.VMEM((k, tm, d), ys.dtype), pltpu.SemaphoreType.DMA],
        compiler_params=_cparams(("arbitrary",)),
        name="moe_combine",
    )(pos.reshape(-1), x, wts, ys)


def route(logits, b_group, b_expert, dm):
    ng, epg = dm.n_groups, dm.epg
    n = logits.shape[0]
    g_prob = jax.nn.softmax(logits[:, :ng] + b_group.astype(F32), axis=-1)
    g_top, g_idx = _top_k(g_prob, 1)
    e_logits = (logits[:, ng : ng + ng * epg] + b_expert.astype(F32)).reshape(n, ng, epg)
    in_group = jnp.take_along_axis(e_logits, g_idx[:, :, None], axis=1)[:, 0]
    e_top, e_idx = _top_k(in_group, dm.top_k)
    weights = g_top * jax.nn.softmax(e_top, axis=-1)
    return (g_idx * epg + e_idx).astype(jnp.int32), weights


def _top_k(x, k):
    cols = lax.broadcasted_iota(jnp.int32, x.shape, 1)
    vals, idxs = [], []
    for _ in range(k):
        i = jnp.argmax(x, axis=-1).astype(jnp.int32)[:, None]
        vals.append(jnp.take_along_axis(x, i, axis=-1))
        idxs.append(i)
        x = jnp.where(cols == i, -jnp.inf, x)
    return jnp.concatenate(vals, axis=-1), jnp.concatenate(idxs, axis=-1)


def dispatch_plan(expert_ids, weights, valid_rows, n_rows, dm):
    ne, k, tm = dm.n_experts, dm.top_k, dm.moe_rows
    i32 = jnp.int32
    n_tok = valid_rows.shape[0]
    n_slots = n_tok * k
    vr = jnp.asarray(valid_rows, i32)
    flat_e = expert_ids[vr].reshape(-1)
    onehot = (flat_e[:, None] == jnp.arange(ne, dtype=i32)[None, :]).astype(i32)
    counts = jnp.sum(onehot, axis=0)
    rank = jnp.take_along_axis(jnp.cumsum(onehot, axis=0), flat_e[:, None], axis=1)[:, 0] - 1
    padded = (counts + tm - 1) // tm * tm
    pend = jnp.cumsum(padded)
    pstart = pend - padded
    dest = pstart[flat_e] + rank
    nblk = -(-n_slots // tm) + ne
    cap = nblk * tm
    slot_row = jnp.repeat(vr, k)
    row_of = jnp.zeros((cap,), i32).at[dest].set(slot_row)
    n_used = (pend[-1] // tm).astype(i32)
    blk_start = jnp.arange(nblk, dtype=i32) * tm
    be = jnp.minimum(jnp.searchsorted(pend, blk_start, side="right"), ne - 1).astype(i32)
    last_used = be[jnp.maximum(n_used - 1, 0)]
    be = jnp.where(jnp.arange(nblk) < n_used, be, last_used)
    first = jnp.concatenate([jnp.ones((1,), i32), (be[1:] != be[:-1]).astype(i32)])
    pos = jnp.zeros((n_rows, k), i32).at[vr].set(dest.reshape(n_tok, k))
    wts = jnp.zeros((n_rows, k), F32).at[vr].set(weights[vr].astype(F32))
    return dict(row_of=row_of, blk_expert=be, blk_first=first, n_used=n_used.reshape(1), pos=pos, wts=wts)


def moe_ffn(x, h, logits, p, l, valid_rows, dm):
    ids, weights = route(logits, p["b_router_group"][l], p["b_router_expert"][l], dm)
    plan = dispatch_plan(ids, weights, valid_rows, x.shape[0], dm)
    xs = gather_rows(h, plan["row_of"], plan["n_used"] * dm.moe_rows, BF16)
    hs = expert_up(xs, p["w_gate"], p["w_up"], l, plan["blk_expert"], plan["blk_first"], plan["n_used"], dm)
    ys = expert_down(hs, p["w_down"], l, plan["blk_expert"], plan["blk_first"], plan["n_used"], dm)
    return moe_combine(x, ys, plan["pos"], plan["wts"], dm)


def rope_tables(tk, dm):
    half = dm.rope // 2
    pos = jnp.arange(tk.tp, dtype=F32)
    inv = dm.rope_theta ** (-jnp.arange(half, dtype=F32) / half)
    ang = pos[:, None] * inv[None, :]
    cos, sin = jnp.cos(ang), jnp.sin(ang)
    z = jnp.zeros((tk.tp, LANE - 2 * half), F32)
    zh = jnp.zeros((tk.tp, half), F32)
    tabs = (jnp.concatenate([cos, cos, z], axis=1),
            jnp.concatenate([-sin, zh, z], axis=1),
            jnp.concatenate([zh, sin, z], axis=1))
    return tuple(jnp.tile(t, (tk.batch, 1)) for t in tabs)


def layer_weights(p, l, dm):
    o1, o2 = dm.q_lora, dm.q_lora + dm.kv_lora
    o3 = o2 + dm.rope
    w_in = p["w_in"][l]
    w_lat = jnp.pad(w_in[:, :o3], ((0, 0), (0, LANE - dm.rope))).astype(BF16)
    w_hy = w_in[:, o3:].astype(BF16)
    nh, qk, hp = dm.n_heads, dm.qk_dim, dm.head_pad
    wq = jnp.pad(p["w_uq"][l].reshape(dm.q_lora, nh, qk), ((0, 0), (0, 0), (0, hp - qk)))
    wq = wq.reshape(dm.q_lora, nh * hp).astype(BF16)
    gq = jnp.pad(p["q_norm_g"][l].astype(F32), (0, hp - qk)).reshape(1, hp)
    gk = jnp.pad(p["k_norm_g"][l].astype(F32), (0, hp - qk)).reshape(1, hp)
    w_router = jnp.concatenate([p["w_router_group"][l], p["w_router_expert"][l]], axis=1)
    return dict(w_lat=w_lat, w_hy=w_hy, wq=wq, wk=p["w_uk"][l].astype(BF16), wv=p["w_uv"][l].astype(BF16),
                gq=gq, gk=gk, w_out=p["w_out"][l].astype(BF16), w_router=w_router)


def encoder(x_list, p, dm):
    d, nm = dm.d_model, dm.n_meta
    depth = p["norm_mix_g"].shape[0]
    trunks, segments, row0 = [], [], 0
    for x in x_list:
        b, ln, _ = x.shape
        t = ln + nm
        tp = _round_up(t + 1, ROW_TILE)
        trunks.append(Trunk(batch=b, t=t, tp=tp, row0=row0))
        segments += [(row0 + s * tp, t) for s in range(b)]
        row0 += b * tp
    segments = tuple(segments)
    n_rows = row0
    valid_rows = np.concatenate([np.arange(s, s + ln) for s, ln in segments]).astype(np.int32)

    meta = p["meta_tokens"].astype(F32)
    pieces = []
    for x, tk in zip(x_list, trunks):
        mt = jnp.broadcast_to(meta[None], (tk.batch, nm, d))
        pad = jnp.zeros((tk.batch, tk.tp - tk.t, d), F32)
        pieces.append(jnp.concatenate([mt, x.astype(F32), pad], axis=1).reshape(tk.rows, d))
    x = jnp.concatenate(pieces, axis=0)

    ropes = [rope_tables(tk, dm) for tk in trunks]
    dfts = [dft_tables(tk) for tk in trunks]
    tm, tn = dm.mm_rows, dm.mm_cols
    for l in range(depth):
        w = layer_weights(p, l, dm)
        h = rmsnorm(x, p["norm_mix_g"][l])
        cq, ckv, kpe = latent_proj(h, w["w_lat"], p["q_latent_g"][l], p["kv_latent_g"][l], dm)
        u = matmul(h, w["w_hy"], F32, tm, min(tn, w["w_hy"].shape[1]))
        mixed = []
        for tk, rt, tabs in zip(trunks, ropes, dfts):
            q, k, v = qkv_heads(cq, ckv, kpe, rt, w["wq"], w["wk"], w["wv"], w["gq"], w["gk"], dm, tk)
            a = attention(q, k, v, dm, tk)
            hy = hyena(u, p, l, tk, tabs, dm)
            mixed.append(group_norm(a, hy, p["out_norm_g"][l], dm.out_group))
        mixed = jnp.concatenate(mixed, axis=0)
        x = matmul_residual(mixed, w["w_out"], x, segments, tm, min(tn, d))
        h, logits = rmsnorm_router(x, p["norm_ffn_g"][l], w["w_router"])
        x = moe_ffn(x, h, logits, p, l, valid_rows, dm)

    outs = []
    for xin, tk in zip(x_list, trunks):
        y = x[tk.row0 : tk.row0 + tk.rows].reshape(tk.batch, tk.tp, d)
        outs.append(y[:, nm : tk.t].astype(xin.dtype))
    return tuple(outs)


def kernel(x_prompt, x_sample, meta_tokens, norm_mix_g, w_in, q_latent_g, kv_latent_g, w_uq, w_uk, w_uv, q_norm_g, k_norm_g, conv_w, conv_b, filt_w1, filt_b1, filt_freq1, filt_w2, filt_b2, filt_freq2, filt_w3, hyena_d, out_norm_g, w_out, norm_ffn_g, w_router_group, b_router_group, w_router_expert, b_router_expert, w_gate, w_up, w_down):
    p = dict(meta_tokens=meta_tokens, norm_mix_g=norm_mix_g, w_in=w_in, q_latent_g=q_latent_g,
             kv_latent_g=kv_latent_g, w_uq=w_uq, w_uk=w_uk, w_uv=w_uv, q_norm_g=q_norm_g, k_norm_g=k_norm_g,
             conv_w=conv_w, conv_b=conv_b, filt_w1=filt_w1, filt_b1=filt_b1, filt_freq1=filt_freq1,
             filt_w2=filt_w2, filt_b2=filt_b2, filt_freq2=filt_freq2, filt_w3=filt_w3, hyena_d=hyena_d,
             out_norm_g=out_norm_g, w_out=w_out, norm_ffn_g=norm_ffn_g, w_router_group=w_router_group,
             b_router_group=b_router_group, w_router_expert=w_router_expert, b_router_expert=b_router_expert,
             w_gate=w_gate, w_up=w_up, w_down=w_down)
    return encoder((x_prompt, x_sample), p, Dims())
```

```python
import dataclasses
import functools
import math

import jax
import jax.numpy as jnp
import numpy as np
from jax import lax
from jax.experimental import pallas as pl
from jax.experimental.pallas import tpu as pltpu

F32 = jnp.float32
BF16 = jnp.bfloat16
EPS = 1e-6
NEG_BIG = -1e30

LANE = 128
SUBLANE = 8
V7X_VMEM_BYTES = 64 * 1024 * 1024
VMEM_LIMIT = V7X_VMEM_BYTES - 8 * 1024 * 1024

ROW_TILE = 256
TIME_BLOCK = LANE


@dataclasses.dataclass(frozen=True)
class Dims:
    d_model: int = 4096
    n_meta: int = 16
    n_heads: int = 16
    nope: int = 128
    rope: int = 64
    v_dim: int = 128
    q_lora: int = 1024
    kv_lora: int = 512
    hy_width: int = 2048
    hy_order: int = 2
    out_group: int = 128
    filt_emb: int = 33
    filt_hidden: int = 64
    n_groups: int = 8
    epg: int = 8
    top_k: int = 2
    d_expert: int = 1024
    rope_theta: float = 10000.0
    fast_decay: float = 0.3
    slow_decay: float = 1.5
    decay_target: float = 1e-2
    moe_rows: int = 512
    moe_up_cols: int = 512
    moe_down_cols: int = 2048
    mm_rows: int = 512
    mm_cols: int = 1024
    dft_cols: int = 512

    @property
    def qk_dim(self):
        return self.nope + self.rope

    @property
    def head_pad(self):
        return _round_up(self.qk_dim, LANE)

    @property
    def n_experts(self):
        return self.n_groups * self.epg


@dataclasses.dataclass(frozen=True)
class Trunk:
    batch: int
    t: int
    tp: int
    row0: int

    @property
    def rows(self):
        return self.batch * self.tp

    @property
    def kh(self):
        return self.tp // TIME_BLOCK

    @property
    def n1(self):
        return _round_up(-(-(2 * self.t - 1) // TIME_BLOCK), SUBLANE)

    @property
    def n(self):
        return self.n1 * TIME_BLOCK


def _round_up(x, m):
    return -(-x // m) * m


def _cparams(sem):
    return pltpu.CompilerParams(dimension_semantics=sem, vmem_limit_bytes=VMEM_LIMIT)


def _dot(a, b):
    return jnp.dot(a, b, preferred_element_type=F32)


def _split_bf16(x):
    hi = x.astype(BF16)
    lo = (x - hi.astype(F32)).astype(BF16)
    return hi, lo


def _dot3(x, w_hi, w_lo):
    x_hi, x_lo = _split_bf16(x)
    return _dot(x_hi, w_hi) + _dot(x_hi, w_lo) + _dot(x_lo, w_hi)


def _seq_valid(rows, segments):
    ok = None
    for start, length in segments:
        m = (rows >= start) & (rows < start + length)
        ok = m if ok is None else (ok | m)
    return ok


def _rmsnorm_kernel(x_ref, g_ref, o_ref):
    x = x_ref[...]
    ms = jnp.mean(x * x, axis=-1, keepdims=True)
    o_ref[...] = (x * lax.rsqrt(ms + EPS) * g_ref[...]).astype(o_ref.dtype)


def rmsnorm(x, g, tm=ROW_TILE):
    r, d = x.shape
    return pl.pallas_call(
        _rmsnorm_kernel,
        out_shape=jax.ShapeDtypeStruct((r, d), BF16),
        grid=(r // tm,),
        in_specs=[pl.BlockSpec((tm, d), lambda i: (i, 0)), pl.BlockSpec((1, d), lambda i: (0, 0))],
        out_specs=pl.BlockSpec((tm, d), lambda i: (i, 0)),
        compiler_params=_cparams(("parallel",)),
        name="rmsnorm",
    )(x, g.reshape(1, d))


def _rmsnorm_router_kernel(x_ref, g_ref, whi_ref, wlo_ref, h_ref, lg_ref):
    x = x_ref[...]
    ms = jnp.mean(x * x, axis=-1, keepdims=True)
    h = x * lax.rsqrt(ms + EPS) * g_ref[...]
    _store_row_slabs(h_ref, h)
    lg_ref[...] = _dot3(h, whi_ref[...], wlo_ref[...])


def _store_row_slabs(slab_ref, v):
    for c in range(slab_ref.shape[1]):
        slab_ref[:, c, :] = v[:, c * LANE : (c + 1) * LANE].astype(slab_ref.dtype)


def rmsnorm_router(x, g, w_router, tm=ROW_TILE):
    r, d = x.shape
    nc = _round_up(w_router.shape[1], LANE)
    w = jnp.pad(w_router.astype(F32), ((0, 0), (0, nc - w_router.shape[1])))
    w_hi, w_lo = _split_bf16(w)
    return pl.pallas_call(
        _rmsnorm_router_kernel,
        out_shape=(jax.ShapeDtypeStruct((r, d // LANE, LANE), F32), jax.ShapeDtypeStruct((r, nc), F32)),
        grid=(r // tm,),
        in_specs=[
            pl.BlockSpec((tm, d), lambda i: (i, 0)),
            pl.BlockSpec((1, d), lambda i: (0, 0)),
            pl.BlockSpec((d, nc), lambda i: (0, 0)),
            pl.BlockSpec((d, nc), lambda i: (0, 0)),
        ],
        out_specs=(pl.BlockSpec((tm, d // LANE, LANE), lambda i: (i, 0, 0)),
                   pl.BlockSpec((tm, nc), lambda i: (i, 0))),
        compiler_params=_cparams(("parallel",)),
        name="rmsnorm_router",
    )(x, g.reshape(1, d), w_hi, w_lo)


def _mm_kernel(a_ref, b_ref, o_ref):
    o_ref[...] = _dot(a_ref[...], b_ref[...]).astype(o_ref.dtype)


def matmul(a, b, out_dtype, tm, tn):
    m, k = a.shape
    n = b.shape[1]
    return pl.pallas_call(
        _mm_kernel,
        out_shape=jax.ShapeDtypeStruct((m, n), out_dtype),
        grid=(m // tm, n // tn),
        in_specs=[pl.BlockSpec((tm, k), lambda i, j: (i, 0)), pl.BlockSpec((k, tn), lambda i, j: (0, j))],
        out_specs=pl.BlockSpec((tm, tn), lambda i, j: (i, j)),
        compiler_params=_cparams(("parallel", "parallel")),
        name="matmul",
    )(a, b)


def _mm_residual_kernel(a_ref, b_ref, r_ref, o_ref, *, tm, segments):
    acc = _dot(a_ref[...], b_ref[...])
    rows = pl.program_id(0) * tm + lax.broadcasted_iota(jnp.int32, (tm, 1), 0)
    o_ref[...] = jnp.where(_seq_valid(rows, segments), r_ref[...] + acc, 0.0)


def matmul_residual(a, b, res, segments, tm, tn):
    m, k = a.shape
    n = b.shape[1]
    return pl.pallas_call(
        functools.partial(_mm_residual_kernel, tm=tm, segments=segments),
        out_shape=jax.ShapeDtypeStruct((m, n), F32),
        grid=(m // tm, n // tn),
        in_specs=[
            pl.BlockSpec((tm, k), lambda i, j: (i, 0)),
            pl.BlockSpec((k, tn), lambda i, j: (0, j)),
            pl.BlockSpec((tm, tn), lambda i, j: (i, j)),
        ],
        out_specs=pl.BlockSpec((tm, tn), lambda i, j: (i, j)),
        compiler_params=_cparams(("parallel", "parallel")),
        name="matmul_residual",
    )(a, b, res)


def _latent_kernel(h_ref, w_ref, gq_ref, gkv_ref, cq_ref, ckv_ref, kpe_ref, *, q_lora, kv_lora):
    p = _dot(h_ref[...], w_ref[...])
    q = p[:, :q_lora]
    kv = p[:, q_lora : q_lora + kv_lora]
    cq_ref[...] = (q * lax.rsqrt(jnp.mean(q * q, axis=-1, keepdims=True) + EPS) * gq_ref[...]).astype(cq_ref.dtype)
    ckv_ref[...] = (kv * lax.rsqrt(jnp.mean(kv * kv, axis=-1, keepdims=True) + EPS) * gkv_ref[...]).astype(
        ckv_ref.dtype
    )
    kpe_ref[...] = p[:, q_lora + kv_lora :]


def latent_proj(h, w_lat, gq, gkv, dm, tm=ROW_TILE):
    r, d = h.shape
    nl = w_lat.shape[1]
    return pl.pallas_call(
        functools.partial(_latent_kernel, q_lora=dm.q_lora, kv_lora=dm.kv_lora),
        out_shape=(
            jax.ShapeDtypeStruct((r, dm.q_lora), BF16),
            jax.ShapeDtypeStruct((r, dm.kv_lora), BF16),
            jax.ShapeDtypeStruct((r, LANE), F32),
        ),
        grid=(r // tm,),
        in_specs=[
            pl.BlockSpec((tm, d), lambda i: (i, 0)),
            pl.BlockSpec((d, nl), lambda i: (0, 0)),
            pl.BlockSpec((1, dm.q_lora), lambda i: (0, 0)),
            pl.BlockSpec((1, dm.kv_lora), lambda i: (0, 0)),
        ],
        out_specs=(
            pl.BlockSpec((tm, dm.q_lora), lambda i: (i, 0)),
            pl.BlockSpec((tm, dm.kv_lora), lambda i: (i, 0)),
            pl.BlockSpec((tm, LANE), lambda i: (i, 0)),
        ),
        compiler_params=_cparams(("parallel",)),
        name="latent_proj",
    )(h, w_lat, gq.reshape(1, -1), gkv.reshape(1, -1))


def _rope_tail(c, cos_t, sin_a, sin_b, half):
    return c * cos_t + pltpu.roll(c, LANE - half, 1) * sin_a + pltpu.roll(c, half, 1) * sin_b


def _qkv_kernel(cq_ref, ckv_ref, kpe_ref, cos_ref, sa_ref, sb_ref, wq_ref, wk_ref, wv_ref, gq_ref, gk_ref,
                q_ref, k_ref, v_ref, *, dm):
    hp, nope, qk = dm.head_pad, dm.nope, dm.qk_dim
    half = dm.rope // 2
    scale = qk ** -0.5
    ckv = ckv_ref[...]
    q = _dot(cq_ref[...], wq_ref[...])
    kn = _dot(ckv, wk_ref[...])
    v_ref[...] = _dot(ckv, wv_ref[...]).astype(v_ref.dtype)
    pe = kpe_ref[...]
    pe_ss = jnp.sum(pe * pe, axis=-1, keepdims=True)
    cos_t, sin_a, sin_b = cos_ref[...], sa_ref[...], sb_ref[...]
    gq = gq_ref[...]
    gk = gk_ref[...]
    for h in range(dm.n_heads):
        qh = q[:, h * hp : (h + 1) * hp]
        r = lax.rsqrt(jnp.sum(qh * qh, axis=-1, keepdims=True) * (1.0 / qk) + EPS) * scale
        qa = qh[:, :nope] * r * gq[:, :nope]
        qc = _rope_tail(qh[:, nope:] * r * gq[:, nope:], cos_t, sin_a, sin_b, half)
        q_ref[:, h * hp : h * hp + nope] = qa.astype(q_ref.dtype)
        q_ref[:, h * hp + nope : (h + 1) * hp] = qc.astype(q_ref.dtype)
        kh = kn[:, h * nope : (h + 1) * nope]
        rk = lax.rsqrt((jnp.sum(kh * kh, axis=-1, keepdims=True) + pe_ss) * (1.0 / qk) + EPS)
        ka = kh * rk * gk[:, :nope]
        kc = _rope_tail(pe * rk * gk[:, nope:], cos_t, sin_a, sin_b, half)
        k_ref[:, h * hp : h * hp + nope] = ka.astype(k_ref.dtype)
        k_ref[:, h * hp + nope : (h + 1) * hp] = kc.astype(k_ref.dtype)


def qkv_heads(cq, ckv, kpe, rope_tabs, wq, wk, wv, gq, gk, dm, tk, tm=ROW_TILE):
    off = tk.row0 // tm
    hp, nh = dm.head_pad, dm.n_heads
    row = lambda i: (off + i, 0)
    loc = lambda i: (i, 0)
    fixed = lambda i: (0, 0)
    return pl.pallas_call(
        functools.partial(_qkv_kernel, dm=dm),
        out_shape=(
            jax.ShapeDtypeStruct((tk.rows, nh * hp), BF16),
            jax.ShapeDtypeStruct((tk.rows, nh * hp), BF16),
            jax.ShapeDtypeStruct((tk.rows, nh * dm.v_dim), BF16),
        ),
        grid=(tk.rows // tm,),
        in_specs=[
            pl.BlockSpec((tm, dm.q_lora), row),
            pl.BlockSpec((tm, dm.kv_lora), row),
            pl.BlockSpec((tm, LANE), row),
            pl.BlockSpec((tm, LANE), loc),
            pl.BlockSpec((tm, LANE), loc),
            pl.BlockSpec((tm, LANE), loc),
            pl.BlockSpec(wq.shape, fixed),
            pl.BlockSpec(wk.shape, fixed),
            pl.BlockSpec(wv.shape, fixed),
            pl.BlockSpec((1, hp), fixed),
            pl.BlockSpec((1, hp), fixed),
        ],
        out_specs=(
            pl.BlockSpec((tm, nh * hp), loc),
            pl.BlockSpec((tm, nh * hp), loc),
            pl.BlockSpec((tm, nh * dm.v_dim), loc),
        ),
        compiler_params=_cparams(("parallel",)),
        name="qkv_heads",
    )(cq, ckv, kpe, *rope_tabs, wq, wk, wv, gq, gk)


def _attn_kernel(q_ref, k_ref, v_ref, o_ref, *, t, tk, tk_last, unroll):
    q = q_ref[...]
    tq = q.shape[0]
    n_full, rem = divmod(t, tk)

    def chunk(start, size, carry, valid):
        m, l, acc = carry
        k = k_ref[pl.ds(start, size), :]
        v = v_ref[pl.ds(start, size), :]
        s = lax.dot_general(q, k, (((1,), (1,)), ((), ())), preferred_element_type=F32)
        if valid < size:
            col = lax.broadcasted_iota(jnp.int32, s.shape, 1)
            s = jnp.where(col < valid, s, NEG_BIG)
        m_new = jnp.maximum(m, jnp.max(s, axis=-1, keepdims=True))
        alpha = jnp.exp(m - m_new)
        p = jnp.exp(s - m_new)
        l = alpha * l + jnp.sum(p, axis=-1, keepdims=True)
        acc = alpha * acc + _dot(p.astype(v.dtype), v)
        return m_new, l, acc

    carry = (jnp.full((tq, 1), NEG_BIG, F32), jnp.zeros((tq, 1), F32), jnp.zeros((tq, v_ref.shape[1]), F32))
    carry = lax.fori_loop(
        0, n_full, lambda c, cr: chunk(pl.multiple_of(c * tk, tk), tk, cr, tk), carry, unroll=unroll)
    if rem:
        carry = chunk(n_full * tk, tk_last, carry, rem)
    _, l, acc = carry
    o_ref[...] = (acc / l).astype(o_ref.dtype)


def _attn_tiles(tk, max_rows, tkc):
    nq = next(n for n in range(1, tk.tp + 1)
              if tk.tp % n == 0 and (tk.tp // n) % (2 * SUBLANE) == 0 and tk.tp // n <= max_rows)
    n_full, rem = divmod(tk.t, tkc)
    tk_last = min(tkc, _round_up(rem, ROW_TILE)) if rem else 0
    assert n_full * tkc + tk_last <= tk.tp
    return nq, tk_last


def attention(q, k, v, dm, tk, max_rows=1152, tkc=512, unroll=2):
    nq, tk_last = _attn_tiles(tk, max_rows, tkc)
    tq = tk.tp // nq
    hp = dm.head_pad
    return pl.pallas_call(
        functools.partial(_attn_kernel, t=tk.t, tk=tkc, tk_last=tk_last, unroll=unroll),
        out_shape=jax.ShapeDtypeStruct((tk.rows, dm.n_heads * dm.v_dim), F32),
        grid=(tk.batch, dm.n_heads, nq),
        in_specs=[
            pl.BlockSpec((tq, hp), lambda b, h, i: (b * nq + i, h)),
            pl.BlockSpec((tk.tp, hp), lambda b, h, i: (b, h)),
            pl.BlockSpec((tk.tp, dm.v_dim), lambda b, h, i: (b, h)),
        ],
        out_specs=pl.BlockSpec((tq, dm.v_dim), lambda b, h, i: (b * nq + i, h)),
        compiler_params=_cparams(("parallel", "parallel", "arbitrary")),
        name="attention",
    )(q, k, v)


def _group_norm_kernel(a_ref, hy_ref, g_ref, o_ref, *, group):
    wa = a_ref.shape[1]
    g = g_ref[...]
    for src, base in ((a_ref, 0), (hy_ref, wa)):
        for c in range(src.shape[1] // group):
            xc = src[:, c * group : (c + 1) * group]
            y = xc * lax.rsqrt(jnp.mean(xc * xc, axis=-1, keepdims=True) + EPS)
            lo = base + c * group
            o_ref[:, lo : lo + group] = (y * g[:, lo : lo + group]).astype(o_ref.dtype)


def group_norm(a, hy, g, group, tm=ROW_TILE):
    r, wa = a.shape
    wh = hy.shape[1]
    return pl.pallas_call(
        functools.partial(_group_norm_kernel, group=group),
        out_shape=jax.ShapeDtypeStruct((r, wa + wh), BF16),
        grid=(r // tm,),
        in_specs=[
            pl.BlockSpec((tm, wa), lambda i: (i, 0)),
            pl.BlockSpec((tm, wh), lambda i: (i, 0)),
            pl.BlockSpec((1, wa + wh), lambda i: (0, 0)),
        ],
        out_specs=pl.BlockSpec((tm, wa + wh), lambda i: (i, 0)),
        compiler_params=_cparams(("parallel",)),
        name="group_norm",
    )(a, hy, g.reshape(1, -1))


def _conv3_kernel(u_ref, prev_ref, next_ref, w_ref, b_ref, o_ref, *, tm, t, tp, starts_at_row0):
    i = pl.program_id(0)
    u = u_ref[...]
    r = lax.broadcasted_iota(jnp.int32, (tm, 1), 0)
    prev_row = prev_ref[SUBLANE - 1 : SUBLANE, :]
    if starts_at_row0:
        prev_row = jnp.where(i == 0, 0.0, prev_row)
    up = jnp.where(r == 0, prev_row, pltpu.roll(u, 1, 0))
    dn = jnp.where(r == tm - 1, next_ref[0:1, :], pltpu.roll(u, tm - 1, 0))
    w = w_ref[...]
    uc = up * w[0:1, :] + u * w[1:2, :] + dn * w[2:3, :] + b_ref[...]
    pos = (i % (tp // tm)) * tm + r
    o_ref[...] = jnp.where(pos < t, uc, 0.0)


def conv3(u, w, b, tk, n_split, tm=ROW_TILE):
    cols = u.shape[1]
    tc = cols // n_split
    off = tk.row0 // tm
    sub = tm // SUBLANE
    last = u.shape[0] // SUBLANE - 1
    return pl.pallas_call(
        functools.partial(_conv3_kernel, tm=tm, t=tk.t, tp=tk.tp, starts_at_row0=(tk.row0 == 0)),
        out_shape=jax.ShapeDtypeStruct((n_split, tk.rows, tc), F32),
        grid=(tk.rows // tm, n_split),
        in_specs=[
            pl.BlockSpec((tm, tc), lambda i, j: (off + i, j)),
            pl.BlockSpec((SUBLANE, tc), lambda i, j: (jnp.maximum((off + i) * sub - 1, 0), j)),
            pl.BlockSpec((SUBLANE, tc), lambda i, j: (jnp.minimum((off + i + 1) * sub, last), j)),
            pl.BlockSpec((3, tc), lambda i, j: (0, j)),
            pl.BlockSpec((1, tc), lambda i, j: (0, j)),
        ],
        out_specs=pl.BlockSpec((None, tm, tc), lambda i, j: (j, i, 0)),
        compiler_params=_cparams(("parallel", "parallel")),
        name="conv3",
    )(u, u, u, w, b.reshape(1, -1))


def _taps_kernel(fvec_ref, w1h_ref, w1l_ref, b1_ref, f1_ref, w2h_ref, w2l_ref, b2_ref, f2_ref, w3h_ref, w3l_ref,
                 dl_ref, o_ref, *, tr, t, n, n_orders, bands):
    p = pl.program_id(0) * tr + lax.broadcasted_iota(jnp.int32, (tr, 1), 0)
    fwd = p < t
    bwd = p > n - t
    lag = jnp.where(fwd | bwd, jnp.where(fwd, p, n - p), 0).astype(F32)
    t_norm = lag * (1.0 / (t - 1))
    arg = ((2.0 * math.pi) * lag / t) * fvec_ref[...]
    lane = lax.broadcasted_iota(jnp.int32, (1, LANE), 1)
    cos_l = (lane >= 1) & (lane <= bands)
    sin_l = (lane > bands) & (lane <= 2 * bands)
    feats = jnp.where(lane == 0, t_norm, jnp.where(cos_l, jnp.cos(arg), jnp.where(sin_l, -jnp.sin(arg), 0.0)))
    hid = jnp.sin(f1_ref[...] * (_dot3(feats, w1h_ref[...], w1l_ref[...]) + b1_ref[...]))
    hid = jnp.sin(f2_ref[...] * (_dot3(hid, w2h_ref[...], w2l_ref[...]) + b2_ref[...]))
    decay = jnp.exp(-t_norm * dl_ref[...])
    for o in range(n_orders):
        f_fwd = _dot3(hid, w3h_ref[2 * o], w3l_ref[2 * o])
        f_bwd = _dot3(hid, w3h_ref[2 * o + 1], w3l_ref[2 * o + 1])
        o_ref[o] = jnp.where(fwd, f_fwd, jnp.where(bwd, f_bwd, 0.0)) * decay


def hyena_taps(fp, tk, dm, tr=ROW_TILE, tc=2048):
    n = tk.n
    c = dm.hy_width
    tc = min(tc, c)
    no = dm.hy_order
    fixed2 = lambda i, j: (0, 0)
    sq = pl.BlockSpec((LANE, LANE), fixed2)
    vec = pl.BlockSpec((1, LANE), fixed2)
    return pl.pallas_call(
        functools.partial(_taps_kernel, tr=tr, t=tk.t, n=n, n_orders=no, bands=(dm.filt_emb - 1) // 2),
        out_shape=jax.ShapeDtypeStruct((no, n, c), F32),
        grid=(n // tr, c // tc),
        in_specs=[vec, sq, sq, vec, vec, sq, sq, vec, vec,
                  pl.BlockSpec((2 * no, LANE, tc), lambda i, j: (0, 0, j)),
                  pl.BlockSpec((2 * no, LANE, tc), lambda i, j: (0, 0, j)),
                  pl.BlockSpec((1, tc), lambda i, j: (0, j))],
        out_specs=pl.BlockSpec((no, tr, tc), lambda i, j: (0, i, j)),
        compiler_params=_cparams(("parallel", "parallel")),
        name="hyena_taps",
    )(fp["fvec"], fp["w1h"], fp["w1l"], fp["b1"], fp["f1"], fp["w2h"], fp["w2l"], fp["b2"], fp["f2"], fp["w3h"],
      fp["w3l"], fp["deltas"])


def _pdft_fwd_kernel(g_ref, z_ref, ar_ref, ai_ref, xs_ref, *, parts, rows, n1):
    xs_ref[...] = jnp.zeros_like(xs_ref)
    half = xs_ref.shape[0] // parts
    for r in range(SUBLANE):
        for p in range(parts):
            xs_ref[p * half : p * half + rows, :] = z_ref[p, :, r, :]
        res = _dot(g_ref[r], xs_ref[...].astype(BF16))
        ar_ref[:, r, :] = res[:n1]
        ai_ref[:, r, :] = res[n1:]


def pdft_fwd(g, z, n1, which, dm):
    s0, sn = which
    _, parts, rows, _, c = z.shape
    cb = min(dm.dft_cols, c)
    kpad = g.shape[2]
    out = jax.ShapeDtypeStruct((sn, n1, TIME_BLOCK, c), F32)
    oblk = pl.BlockSpec((None, n1, SUBLANE, cb), lambda s, j, k: (s, 0, j, k))
    return pl.pallas_call(
        functools.partial(_pdft_fwd_kernel, parts=parts, rows=rows, n1=n1),
        out_shape=(out, out),
        grid=(sn, TIME_BLOCK // SUBLANE, c // cb),
        in_specs=[
            pl.BlockSpec((SUBLANE, 2 * n1, kpad), lambda s, j, k: (j, 0, 0)),
            pl.BlockSpec((None, parts, rows, SUBLANE, cb), lambda s, j, k: (s0 + s, 0, 0, j, k)),
        ],
        out_specs=(oblk, oblk),
        scratch_shapes=[pltpu.VMEM((kpad, cb), F32)],
        compiler_params=_cparams(("parallel", "parallel", "parallel")),
        name="pdft_fwd",
    )(g, z)


def _spectrum_kernel(ar_ref, ai_ref, f_ref, h_ref):
    f = f_ref[...]
    for k in range(ar_ref.shape[0]):
        a = jnp.concatenate([ar_ref[k], ai_ref[k]], axis=0).astype(BF16)
        h_ref[k] = _dot(f, a)


def filter_spectrum(ar, ai, f2f, dm, kb=SUBLANE):
    no, n1, _, c = ar.shape
    cb = min(dm.dft_cols, c)
    blk = pl.BlockSpec((None, kb, TIME_BLOCK, cb), lambda o, i, j: (o, i, 0, j))
    return pl.pallas_call(
        _spectrum_kernel,
        out_shape=jax.ShapeDtypeStruct((no, n1, 2 * TIME_BLOCK, c), F32),
        grid=(no, n1 // kb, c // cb),
        in_specs=[blk, blk, pl.BlockSpec(f2f.shape, lambda o, i, j: (0, 0))],
        out_specs=pl.BlockSpec((None, kb, 2 * TIME_BLOCK, cb), lambda o, i, j: (o, i, 0, j)),
        compiler_params=_cparams(("parallel", "parallel", "parallel")),
        name="filter_spectrum",
    )(ar, ai, f2f)


def _freq_mul_kernel(ar_ref, ai_ref, h_ref, ff_ref, fi_ref, br_ref, bi_ref):
    ff = ff_ref[...]
    fi = fi_ref[...]
    nb = TIME_BLOCK
    for k in range(ar_ref.shape[0]):
        x = _dot(ff, jnp.concatenate([ar_ref[k], ai_ref[k]], axis=0).astype(BF16))
        h = h_ref[k]
        xr, xi = x[:nb], x[nb:]
        hr, hi = h[:nb], h[nb:]
        y = jnp.concatenate([xr * hr - xi * hi, xr * hi + xi * hr], axis=0).astype(BF16)
        b = _dot(fi, y)
        br_ref[k] = b[:nb]
        bi_ref[k] = b[nb:]


def freq_multiply(ar, ai, h, order, f2f, f2i, dm, kb=SUBLANE):
    _, n1, _, c = ar.shape
    cb = min(dm.dft_cols, c)
    ablk = pl.BlockSpec((None, kb, TIME_BLOCK, cb), lambda i, j: (0, i, 0, j))
    blk = pl.BlockSpec((kb, TIME_BLOCK, cb), lambda i, j: (i, 0, j))
    out = jax.ShapeDtypeStruct((n1, TIME_BLOCK, c), F32)
    return pl.pallas_call(
        _freq_mul_kernel,
        out_shape=(out, out),
        grid=(n1 // kb, c // cb),
        in_specs=[ablk, ablk,
                  pl.BlockSpec((None, kb, 2 * TIME_BLOCK, cb), lambda i, j: (order, i, 0, j)),
                  pl.BlockSpec(f2f.shape, lambda i, j: (0, 0)),
                  pl.BlockSpec(f2i.shape, lambda i, j: (0, 0))],
        out_specs=(blk, blk),
        compiler_params=_cparams(("parallel", "parallel")),
        name="freq_multiply",
    )(ar, ai, h, f2f, f2i)


def _pdft_inv_gate_kernel(g_ref, br_ref, bi_ref, gate_ref, z_ref, d_ref, o_ref, bs_ref, *, n1, rows):
    bs_ref[...] = jnp.zeros_like(bs_ref)
    half = bs_ref.shape[0] // 2
    mh = g_ref.shape[1] // 2
    d = d_ref[...]
    for r in range(SUBLANE):
        bs_ref[0:n1, :] = br_ref[:, r, :]
        bs_ref[half : half + n1, :] = bi_ref[:, r, :]
        y = _dot(g_ref[r], bs_ref[...].astype(BF16))
        for b in range(2):
            o_ref[b, :, r, :] = gate_ref[b, :, r, :] * (y[b * mh : b * mh + rows] + d * z_ref[b, :, r, :])


def pdft_inv_gate(ginv, br, bi, sig, gate_idx, z, z_idx, d, tk, dm):
    n1, _, c = br.shape
    cb = min(dm.dft_cols, c)
    kh = tk.kh
    kpad = ginv.shape[2]
    seq = lambda idx: pl.BlockSpec((None, 2, kh, SUBLANE, cb), lambda j, k: (idx, 0, 0, j, k))
    bblk = pl.BlockSpec((n1, SUBLANE, cb), lambda j, k: (0, j, k))
    return pl.pallas_call(
        functools.partial(_pdft_inv_gate_kernel, n1=n1, rows=kh),
        out_shape=jax.ShapeDtypeStruct((1, 2, kh, TIME_BLOCK, c), F32),
        grid=(TIME_BLOCK // SUBLANE, c // cb),
        in_specs=[
            pl.BlockSpec((SUBLANE, ginv.shape[1], kpad), lambda j, k: (j, 0, 0)),
            bblk,
            bblk,
            seq(gate_idx),
            seq(z_idx),
            pl.BlockSpec((1, cb), lambda j, k: (0, k)),
        ],
        out_specs=seq(0),
        scratch_shapes=[pltpu.VMEM((kpad, cb), F32)],
        compiler_params=_cparams(("parallel", "parallel")),
        name="pdft_inv_gate",
    )(ginv, br, bi, sig, z, d.reshape(1, c))


def _dft_real_form(phase_num, n, sign, scale=1.0):
    ang = (sign * 2.0 * math.pi / n) * (phase_num % n).astype(F32)
    c = jnp.cos(ang) * scale
    s = jnp.sin(ang) * scale
    top = jnp.concatenate([c, -s], axis=-1)
    bot = jnp.concatenate([s, c], axis=-1)
    return jnp.concatenate([top, bot], axis=-2).astype(BF16)


def dft_tables(tk):
    n1, n, nb = tk.n1, tk.n, TIME_BLOCK
    i32 = jnp.int32
    n2 = jnp.arange(nb, dtype=i32)[:, None, None]
    k1 = jnp.arange(n1, dtype=i32)[None, :, None]

    def major(cols):
        col = jnp.arange(cols, dtype=i32)[None, None, :]
        return k1 * (nb * col + n2)

    sig_half = _round_up(tk.kh, 64)
    tap_half = _round_up(n1, LANE)
    out_half = _round_up(tk.kh, SUBLANE)
    g_sig = _dft_real_form(major(sig_half), n, -1.0)
    g_tap = _dft_real_form(major(tap_half), n, -1.0)[:, :, :tap_half]
    rows = jnp.arange(out_half, dtype=i32)[None, :, None]
    kcol = jnp.arange(tap_half, dtype=i32)[None, None, :]
    g_inv = _dft_real_form(kcol * (nb * rows + n2), n, 1.0, 1.0 / n)
    kk = jnp.arange(nb, dtype=i32)
    minor = kk[:, None] * kk[None, :]
    f2f = _dft_real_form(minor, nb, -1.0)
    f2i = _dft_real_form(minor, nb, 1.0)
    return dict(g_sig=g_sig, g_tap=g_tap, g_inv=g_inv, f2f=f2f, f2i=f2i)


def filter_params(p, l, dm):
    emb, hid, c, no = dm.filt_emb, dm.filt_hidden, dm.hy_width, dm.hy_order
    bands = (emb - 1) // 2

    def pad2(w, r, cc):
        return jnp.pad(w.astype(F32), ((0, r - w.shape[0]), (0, cc - w.shape[1])))

    def padv(v):
        return jnp.pad(v.astype(F32), (0, LANE - v.shape[0])).reshape(1, LANE)

    freqs = jnp.linspace(1e-4, bands - 1, bands, dtype=F32)
    fvec = jnp.concatenate([jnp.zeros((1,), F32), freqs, freqs, jnp.zeros((LANE - emb,), F32)]).reshape(1, LANE)
    w1h, w1l = _split_bf16(pad2(p["filt_w1"][l], LANE, LANE))
    w2h, w2l = _split_bf16(pad2(p["filt_w2"][l], LANE, LANE))
    w3 = p["filt_w3"][l].astype(F32).reshape(hid, 2 * no, c).transpose(1, 0, 2)
    w3h, w3l = _split_bf16(jnp.pad(w3, ((0, 0), (0, LANE - hid), (0, 0))))
    deltas = jnp.abs(jnp.linspace(math.log(dm.decay_target) / dm.slow_decay,
                                  math.log(dm.decay_target) / dm.fast_decay, c, dtype=F32)).reshape(1, c)
    return dict(fvec=fvec, w1h=w1h, w1l=w1l, b1=padv(p["filt_b1"][l]), f1=padv(p["filt_freq1"][l]),
                w2h=w2h, w2l=w2l, b2=padv(p["filt_b2"][l]), f2=padv(p["filt_freq2"][l]),
                w3h=w3h, w3l=w3l, deltas=deltas)


def hyena(u, p, l, tk, tabs, dm):
    c, no = dm.hy_width, dm.hy_order
    n1, kh = tk.n1, tk.kh
    assert tk.batch == 2, "the two sequences of a trunk are the real and imaginary parts of one transform"
    sig = conv3(u, p["conv_w"][l], p["conv_b"][l], tk, no + 1)
    sig = sig.reshape(no + 1, tk.batch, kh, TIME_BLOCK, c)
    taps = hyena_taps(filter_params(p, l, dm), tk, dm)
    tr, ti = pdft_fwd(tabs["g_tap"], taps.reshape(no, 1, n1, TIME_BLOCK, c), n1, (0, no), dm)
    spec = filter_spectrum(tr, ti, tabs["f2f"], dm)
    z = sig
    for o in range(no):
        ar, ai = pdft_fwd(tabs["g_sig"], z, n1, (0, 1), dm)
        br, bi = freq_multiply(ar, ai, spec, o, tabs["f2f"], tabs["f2i"], dm)
        z = pdft_inv_gate(tabs["g_inv"], br, bi, sig, o + 1, z, 0, p["hyena_d"][l, o], tk, dm)
    return z.reshape(tk.rows, c)


def _row_copy(src_ref, src_row, dst_ref, dst_row, sem):
    return pltpu.make_async_copy(src_ref.at[pl.ds(src_row, 1)], dst_ref.at[pl.ds(dst_row, 1)], sem)


DMA_ISSUE_UNROLL = 8


def _gather_rows_kernel(nrows_ref, idx_ref, src_ref, o_ref, buf_ref, sem, *, tg):
    live = pl.program_id(0) * tg < nrows_ref[0]

    @pl.when(live)
    def _():
        def issue(i, _):
            _row_copy(src_ref, idx_ref[i], buf_ref, i, sem).start()
            return 0

        lax.fori_loop(0, tg, issue, 0, unroll=DMA_ISSUE_UNROLL)

        def drain(i, _):
            _row_copy(src_ref, 0, buf_ref, i, sem).wait()
            return 0

        lax.fori_loop(0, tg, drain, 0, unroll=DMA_ISSUE_UNROLL)
        for c in range(buf_ref.shape[1]):
            o_ref[:, c * LANE : (c + 1) * LANE] = buf_ref[:, c, :].astype(o_ref.dtype)

    @pl.when(jnp.logical_not(live))
    def _():
        o_ref[...] = jnp.zeros_like(o_ref)


def gather_rows(src, idx, n_live, out_dtype, tg=ROW_TILE):
    n = idx.shape[0]
    d = src.shape[1] * src.shape[2]
    return pl.pallas_call(
        functools.partial(_gather_rows_kernel, tg=tg),
        out_shape=jax.ShapeDtypeStruct((n, d), out_dtype),
        grid_spec=pltpu.PrefetchScalarGridSpec(
            num_scalar_prefetch=1,
            grid=(n // tg,),
            in_specs=[
                pl.BlockSpec((tg,), lambda i, nr: (i,), memory_space=pltpu.SMEM),
                pl.BlockSpec(memory_space=pl.ANY),
            ],
            out_specs=pl.BlockSpec((tg, d), lambda i, nr: (i, 0)),
            scratch_shapes=[pltpu.VMEM((tg,) + src.shape[1:], src.dtype), pltpu.SemaphoreType.DMA],
        ),
        compiler_params=_cparams(("arbitrary",)),
        name="moe_gather",
    )(n_live, idx, src)


def _expert_up_kernel(be_ref, first_ref, nb_ref, x_ref, wg_ref, wu_ref, o_ref, wgs_ref, wus_ref):
    m = pl.program_id(1)

    @pl.when(m < nb_ref[0])
    def _():
        @pl.when(first_ref[m] == 1)
        def _():
            wgs_ref[...] = wg_ref[...].astype(BF16)
            wus_ref[...] = wu_ref[...].astype(BF16)

        x = x_ref[...]
        g = _dot(x, wgs_ref[...])
        u = _dot(x, wus_ref[...])
        o_ref[...] = (g * jax.nn.sigmoid(g) * u).astype(o_ref.dtype)

    @pl.when(m >= nb_ref[0])
    def _():
        o_ref[...] = jnp.zeros_like(o_ref)


def expert_up(xs, w_gate, w_up, l, blk_expert, blk_first, n_blocks, dm):
    cap, d = xs.shape
    tm, tf = dm.moe_rows, min(dm.moe_up_cols, dm.d_expert)
    nblk = cap // tm
    wspec = pl.BlockSpec((None, None, d, tf), lambda j, m, be, fi, nb: (l, be[m], 0, j))
    return pl.pallas_call(
        _expert_up_kernel,
        out_shape=jax.ShapeDtypeStruct((cap, dm.d_expert), BF16),
        grid_spec=pltpu.PrefetchScalarGridSpec(
            num_scalar_prefetch=3,
            grid=(dm.d_expert // tf, nblk),
            in_specs=[pl.BlockSpec((tm, d), lambda j, m, be, fi, nb: (m, 0)), wspec, wspec],
            out_specs=pl.BlockSpec((tm, tf), lambda j, m, be, fi, nb: (m, j)),
            scratch_shapes=[pltpu.VMEM((d, tf), BF16), pltpu.VMEM((d, tf), BF16)],
        ),
        compiler_params=_cparams(("arbitrary", "arbitrary")),
        name="moe_expert_up",
    )(blk_expert, blk_first, n_blocks, xs, w_gate, w_up)


def _expert_down_kernel(be_ref, first_ref, nb_ref, h_ref, wd_ref, o_ref, wds_ref):
    m = pl.program_id(1)

    @pl.when(m < nb_ref[0])
    def _():
        @pl.when(first_ref[m] == 1)
        def _():
            wds_ref[...] = wd_ref[...].astype(BF16)

        _store_row_slabs(o_ref, _dot(h_ref[...], wds_ref[...]))

    @pl.when(m >= nb_ref[0])
    def _():
        o_ref[...] = jnp.zeros_like(o_ref)


def expert_down(hs, w_down, l, blk_expert, blk_first, n_blocks, dm):
    cap, f = hs.shape
    d = w_down.shape[3]
    tm, tn = dm.moe_rows, min(dm.moe_down_cols, d)
    nblk = cap // tm
    return pl.pallas_call(
        _expert_down_kernel,
        out_shape=jax.ShapeDtypeStruct((cap, d // LANE, LANE), F32),
        grid_spec=pltpu.PrefetchScalarGridSpec(
            num_scalar_prefetch=3,
            grid=(d // tn, nblk),
            in_specs=[pl.BlockSpec((tm, f), lambda j, m, be, fi, nb: (m, 0)),
                      pl.BlockSpec((None, None, f, tn), lambda j, m, be, fi, nb: (l, be[m], 0, j))],
            out_specs=pl.BlockSpec((tm, tn // LANE, LANE), lambda j, m, be, fi, nb: (m, j, 0)),
            scratch_shapes=[pltpu.VMEM((f, tn), BF16)],
        ),
        compiler_params=_cparams(("arbitrary", "arbitrary")),
        name="moe_expert_down",
    )(blk_expert, blk_first, n_blocks, hs, w_down)


def _combine_kernel(pos_ref, x_ref, w_ref, ys_ref, o_ref, buf_ref, sem, *, tm, top_k):
    def issue(i, _):
        for k in range(top_k):
            pltpu.make_async_copy(ys_ref.at[pl.ds(pos_ref[i * top_k + k], 1)], buf_ref.at[k, pl.ds(i, 1)], sem).start()
        return 0

    lax.fori_loop(0, tm, issue, 0, unroll=DMA_ISSUE_UNROLL // top_k)

    def drain(i, _):
        for k in range(top_k):
            pltpu.make_async_copy(ys_ref.at[pl.ds(0, 1)], buf_ref.at[k, pl.ds(i, 1)], sem).wait()
        return 0

    lax.fori_loop(0, tm, drain, 0, unroll=DMA_ISSUE_UNROLL // top_k)
    w = w_ref[...]
    for c in range(buf_ref.shape[2]):
        acc = x_ref[:, c * LANE : (c + 1) * LANE]
        for k in range(top_k):
            acc = acc + w[:, k : k + 1] * buf_ref[k, :, c, :]
        o_ref[:, c * LANE : (c + 1) * LANE] = acc


def moe_combine(x, ys, pos, wts, dm, tm=ROW_TILE):
    r, d = x.shape
    k = dm.top_k
    return pl.pallas_call(
        functools.partial(_combine_kernel, tm=tm, top_k=k),
        out_shape=jax.ShapeDtypeStruct((r, d), F32),
        grid=(r // tm,),
        in_specs=[
            pl.BlockSpec((tm * k,), lambda i: (i,), memory_space=pltpu.SMEM),
            pl.BlockSpec((tm, d), lambda i: (i, 0)),
            pl.BlockSpec((tm, k), lambda i: (i, 0)),
            pl.BlockSpec(memory_space=pl.ANY),
        ],
        out_specs=pl.BlockSpec((tm, d), lambda i: (i, 0)),
        scratch_shapes=[pltpu.VMEM((k, tm) + ys.shape[1:], ys.dtype), pltpu.SemaphoreType.DMA],
        compiler_params=_cparams(("arbitrary",)),
        name="moe_combine",
    )(pos.reshape(-1), x, wts, ys)


def route(logits, b_group, b_expert, dm):
    ng, epg = dm.n_groups, dm.epg
    n = logits.shape[0]
    g_prob = jax.nn.softmax(logits[:, :ng] + b_group.astype(F32), axis=-1)
    g_top, g_idx = _top_k(g_prob, 1)
    e_logits = (logits[:, ng : ng + ng * epg] + b_expert.astype(F32)).reshape(n, ng, epg)
    in_group = jnp.take_along_axis(e_logits, g_idx[:, :, None], axis=1)[:, 0]
    e_top, e_idx = _top_k(in_group, dm.top_k)
    weights = g_top * jax.nn.softmax(e_top, axis=-1)
    return (g_idx * epg + e_idx).astype(jnp.int32), weights


def _top_k(x, k):
    cols = lax.broadcasted_iota(jnp.int32, x.shape, 1)
    vals, idxs = [], []
    for _ in range(k):
        i = jnp.argmax(x, axis=-1).astype(jnp.int32)[:, None]
        vals.append(jnp.take_along_axis(x, i, axis=-1))
        idxs.append(i)
        x = jnp.where(cols == i, -jnp.inf, x)
    return jnp.concatenate(vals, axis=-1), jnp.concatenate(idxs, axis=-1)


def _real_rows(a, segments):
    return jnp.concatenate([a[s : s + t] for s, t in segments], axis=0)


def _spread_rows(a, segments, n_rows):
    out, off, end = [], 0, 0
    for s, t in segments:
        out += [jnp.zeros((s - end,) + a.shape[1:], a.dtype), a[off : off + t]]
        off, end = off + t, s + t
    out.append(jnp.zeros((n_rows - end,) + a.shape[1:], a.dtype))
    return jnp.concatenate(out, axis=0)


def dispatch_plan(expert_ids, weights, segments, n_rows, dm):
    ne, k, tm = dm.n_experts, dm.top_k, dm.moe_rows
    i32 = jnp.int32
    vr = jnp.asarray(np.concatenate([np.arange(s, s + t) for s, t in segments]), i32)
    n_tok = vr.shape[0]
    n_slots = n_tok * k
    flat_e = _real_rows(expert_ids, segments).reshape(-1)
    onehot = (flat_e[:, None] == jnp.arange(ne, dtype=i32)[None, :]).astype(i32)
    counts = jnp.sum(onehot, axis=0)
    rank = jnp.take_along_axis(jnp.cumsum(onehot, axis=0), flat_e[:, None], axis=1)[:, 0] - 1
    padded = (counts + tm - 1) // tm * tm
    pend = jnp.cumsum(padded)
    pstart = pend - padded
    dest = pstart[flat_e] + rank
    nblk = -(-n_slots // tm) + ne
    cap = nblk * tm
    slot_row = jnp.repeat(vr, k)
    row_of = jnp.zeros((cap,), i32).at[dest].set(slot_row)
    n_used = (pend[-1] // tm).astype(i32)
    blk_start = jnp.arange(nblk, dtype=i32) * tm
    be = jnp.minimum(jnp.searchsorted(pend, blk_start, side="right"), ne - 1).astype(i32)
    last_used = be[jnp.maximum(n_used - 1, 0)]
    be = jnp.where(jnp.arange(nblk) < n_used, be, last_used)
    first = jnp.concatenate([jnp.ones((1,), i32), (be[1:] != be[:-1]).astype(i32)])
    pos = _spread_rows(dest.reshape(n_tok, k), segments, n_rows)
    wts = _spread_rows(_real_rows(weights.astype(F32), segments), segments, n_rows)
    return dict(row_of=row_of, blk_expert=be, blk_first=first, n_used=n_used.reshape(1), pos=pos, wts=wts)


def moe_ffn(x, h, logits, p, l, segments, dm):
    ids, weights = route(logits, p["b_router_group"][l], p["b_router_expert"][l], dm)
    plan = dispatch_plan(ids, weights, segments, x.shape[0], dm)
    xs = gather_rows(h, plan["row_of"], plan["n_used"] * dm.moe_rows, BF16)
    hs = expert_up(xs, p["w_gate"], p["w_up"], l, plan["blk_expert"], plan["blk_first"], plan["n_used"], dm)
    ys = expert_down(hs, p["w_down"], l, plan["blk_expert"], plan["blk_first"], plan["n_used"], dm)
    return moe_combine(x, ys, plan["pos"], plan["wts"], dm)


def rope_tables(tk, dm):
    half = dm.rope // 2
    pos = jnp.arange(tk.tp, dtype=F32)
    inv = dm.rope_theta ** (-jnp.arange(half, dtype=F32) / half)
    ang = pos[:, None] * inv[None, :]
    cos, sin = jnp.cos(ang), jnp.sin(ang)
    z = jnp.zeros((tk.tp, LANE - 2 * half), F32)
    zh = jnp.zeros((tk.tp, half), F32)
    tabs = (jnp.concatenate([cos, cos, z], axis=1),
            jnp.concatenate([-sin, zh, z], axis=1),
            jnp.concatenate([zh, sin, z], axis=1))
    return tuple(jnp.tile(t, (tk.batch, 1)) for t in tabs)


def layer_weights(p, l, dm):
    o1, o2 = dm.q_lora, dm.q_lora + dm.kv_lora
    o3 = o2 + dm.rope
    w_in = p["w_in"][l]
    w_lat = jnp.pad(w_in[:, :o3], ((0, 0), (0, LANE - dm.rope))).astype(BF16)
    w_hy = w_in[:, o3:].astype(BF16)
    nh, qk, hp = dm.n_heads, dm.qk_dim, dm.head_pad
    wq = jnp.pad(p["w_uq"][l].reshape(dm.q_lora, nh, qk), ((0, 0), (0, 0), (0, hp - qk)))
    wq = wq.reshape(dm.q_lora, nh * hp).astype(BF16)
    gq = jnp.pad(p["q_norm_g"][l].astype(F32), (0, hp - qk)).reshape(1, hp)
    gk = jnp.pad(p["k_norm_g"][l].astype(F32), (0, hp - qk)).reshape(1, hp)
    w_router = jnp.concatenate([p["w_router_group"][l], p["w_router_expert"][l]], axis=1)
    return dict(w_lat=w_lat, w_hy=w_hy, wq=wq, wk=p["w_uk"][l].astype(BF16), wv=p["w_uv"][l].astype(BF16),
                gq=gq, gk=gk, w_out=p["w_out"][l].astype(BF16), w_router=w_router)


def encoder(x_list, p, dm):
    d, nm = dm.d_model, dm.n_meta
    depth = p["norm_mix_g"].shape[0]
    trunks, segments, row0 = [], [], 0
    for x in x_list:
        b, ln, _ = x.shape
        t = ln + nm
        tp = _round_up(t + 1, ROW_TILE)
        trunks.append(Trunk(batch=b, t=t, tp=tp, row0=row0))
        segments += [(row0 + s * tp, t) for s in range(b)]
        row0 += b * tp
    segments = tuple(segments)
    n_rows = row0

    meta = p["meta_tokens"].astype(F32)
    pieces = []
    for xin, tk in zip(x_list, trunks):
        pad = jnp.zeros((tk.tp - tk.t, d), F32)
        for b in range(tk.batch):
            pieces += [meta, xin[b].astype(F32), pad]
    x = jnp.concatenate(pieces, axis=0)

    ropes = [rope_tables(tk, dm) for tk in trunks]
    dfts = [dft_tables(tk) for tk in trunks]
    tm, tn = dm.mm_rows, dm.mm_cols
    for l in range(depth):
        w = layer_weights(p, l, dm)
        h = rmsnorm(x, p["norm_mix_g"][l])
        cq, ckv, kpe = latent_proj(h, w["w_lat"], p["q_latent_g"][l], p["kv_latent_g"][l], dm)
        u = matmul(h, w["w_hy"], F32, tm, min(tn, w["w_hy"].shape[1]))
        mixed = []
        for tk, rt, tabs in zip(trunks, ropes, dfts):
            q, k, v = qkv_heads(cq, ckv, kpe, rt, w["wq"], w["wk"], w["wv"], w["gq"], w["gk"], dm, tk)
            a = attention(q, k, v, dm, tk)
            hy = hyena(u, p, l, tk, tabs, dm)
            mixed.append(group_norm(a, hy, p["out_norm_g"][l], dm.out_group))
        mixed = jnp.concatenate(mixed, axis=0)
        x = matmul_residual(mixed, w["w_out"], x, segments, tm, min(tn, d))
        h, logits = rmsnorm_router(x, p["norm_ffn_g"][l], w["w_router"])
        x = moe_ffn(x, h, logits, p, l, segments, dm)

    outs = []
    for xin, tk in zip(x_list, trunks):
        seqs = [x[tk.row0 + b * tk.tp + nm : tk.row0 + b * tk.tp + tk.t] for b in range(tk.batch)]
        outs.append(jnp.stack(seqs, axis=0).astype(xin.dtype))
    return tuple(outs)


def kernel(x_prompt, x_sample, meta_tokens, norm_mix_g, w_in, q_latent_g, kv_latent_g, w_uq, w_uk, w_uv, q_norm_g, k_norm_g, conv_w, conv_b, filt_w1, filt_b1, filt_freq1, filt_w2, filt_b2, filt_freq2, filt_w3, hyena_d, out_norm_g, w_out, norm_ffn_g, w_router_group, b_router_group, w_router_expert, b_router_expert, w_gate, w_up, w_down):
    p = dict(meta_tokens=meta_tokens, norm_mix_g=norm_mix_g, w_in=w_in, q_latent_g=q_latent_g,
             kv_latent_g=kv_latent_g, w_uq=w_uq, w_uk=w_uk, w_uv=w_uv, q_norm_g=q_norm_g, k_norm_g=k_norm_g,
             conv_w=conv_w, conv_b=conv_b, filt_w1=filt_w1, filt_b1=filt_b1, filt_freq1=filt_freq1,
             filt_w2=filt_w2, filt_b2=filt_b2, filt_freq2=filt_freq2, filt_w3=filt_w3, hyena_d=hyena_d,
             out_norm_g=out_norm_g, w_out=w_out, norm_ffn_g=norm_ffn_g, w_router_group=w_router_group,
             b_router_group=b_router_group, w_router_expert=w_router_expert, b_router_expert=b_router_expert,
             w_gate=w_gate, w_up=w_up, w_down=w_down)
    return encoder((x_prompt, x_sample), p, Dims())
```

```python
import dataclasses
import functools
import math

import jax
import jax.numpy as jnp
import numpy as np
from jax import lax
from jax.experimental import pallas as pl
from jax.experimental.pallas import tpu as pltpu

F32 = jnp.float32
BF16 = jnp.bfloat16
EPS = 1e-6
NEG_BIG = -1e30

LANE = 128
SUBLANE = 8
V7X_VMEM_BYTES = 64 * 1024 * 1024
VMEM_LIMIT = V7X_VMEM_BYTES - 8 * 1024 * 1024

ROW_TILE = 256
TIME_BLOCK = LANE


@dataclasses.dataclass(frozen=True)
class Dims:
    d_model: int = 4096
    n_meta: int = 16
    n_heads: int = 16
    nope: int = 128
    rope: int = 64
    v_dim: int = 128
    q_lora: int = 1024
    kv_lora: int = 512
    hy_width: int = 2048
    hy_order: int = 2
    out_group: int = 128
    filt_emb: int = 33
    filt_hidden: int = 64
    n_groups: int = 8
    epg: int = 8
    top_k: int = 2
    d_expert: int = 1024
    rope_theta: float = 10000.0
    fast_decay: float = 0.3
    slow_decay: float = 1.5
    decay_target: float = 1e-2
    moe_rows: int = 512
    moe_up_cols: int = 512
    moe_down_cols: int = 2048
    mm_rows: int = 512
    mm_cols: int = 1024
    dft_cols: int = 512

    @property
    def qk_dim(self):
        return self.nope + self.rope

    @property
    def head_pad(self):
        return _round_up(self.qk_dim, LANE)

    @property
    def n_experts(self):
        return self.n_groups * self.epg


@dataclasses.dataclass(frozen=True)
class Trunk:
    batch: int
    t: int
    tp: int
    row0: int

    @property
    def rows(self):
        return self.batch * self.tp

    @property
    def kh(self):
        return self.tp // TIME_BLOCK

    @property
    def n1(self):
        return _round_up(-(-(2 * self.t - 1) // TIME_BLOCK), SUBLANE)

    @property
    def n(self):
        return self.n1 * TIME_BLOCK


def _round_up(x, m):
    return -(-x // m) * m


def _cparams(sem):
    return pltpu.CompilerParams(dimension_semantics=sem, vmem_limit_bytes=VMEM_LIMIT)


def _dot(a, b):
    return jnp.dot(a, b, preferred_element_type=F32)


def _split_bf16(x):
    hi = x.astype(BF16)
    lo = (x - hi.astype(F32)).astype(BF16)
    return hi, lo


def _dot3(x, w_hi, w_lo):
    x_hi, x_lo = _split_bf16(x)
    return _dot(x_hi, w_hi) + _dot(x_hi, w_lo) + _dot(x_lo, w_hi)


def _seq_valid(rows, segments):
    ok = None
    for start, length in segments:
        m = (rows >= start) & (rows < start + length)
        ok = m if ok is None else (ok | m)
    return ok


def _rmsnorm_kernel(x_ref, g_ref, o_ref):
    x = x_ref[...]
    ms = jnp.mean(x * x, axis=-1, keepdims=True)
    o_ref[...] = (x * lax.rsqrt(ms + EPS) * g_ref[...]).astype(o_ref.dtype)


def rmsnorm(x, g, tm=ROW_TILE):
    r, d = x.shape
    return pl.pallas_call(
        _rmsnorm_kernel,
        out_shape=jax.ShapeDtypeStruct((r, d), BF16),
        grid=(r // tm,),
        in_specs=[pl.BlockSpec((tm, d), lambda i: (i, 0)), pl.BlockSpec((1, d), lambda i: (0, 0))],
        out_specs=pl.BlockSpec((tm, d), lambda i: (i, 0)),
        compiler_params=_cparams(("parallel",)),
        name="rmsnorm",
    )(x, g.reshape(1, d))


def _rmsnorm_router_kernel(x_ref, g_ref, whi_ref, wlo_ref, h_ref, lg_ref):
    x = x_ref[...]
    ms = jnp.mean(x * x, axis=-1, keepdims=True)
    h = x * lax.rsqrt(ms + EPS) * g_ref[...]
    h_ref[...] = h.astype(h_ref.dtype)
    lg_ref[...] = _dot3(h, whi_ref[...], wlo_ref[...])


def rmsnorm_router(x, g, w_router, tm=ROW_TILE):
    r, d = x.shape
    nc = _round_up(w_router.shape[1], LANE)
    w = jnp.pad(w_router.astype(F32), ((0, 0), (0, nc - w_router.shape[1])))
    w_hi, w_lo = _split_bf16(w)
    return pl.pallas_call(
        _rmsnorm_router_kernel,
        out_shape=(jax.ShapeDtypeStruct((r, d), F32), jax.ShapeDtypeStruct((r, nc), F32)),
        grid=(r // tm,),
        in_specs=[
            pl.BlockSpec((tm, d), lambda i: (i, 0)),
            pl.BlockSpec((1, d), lambda i: (0, 0)),
            pl.BlockSpec((d, nc), lambda i: (0, 0)),
            pl.BlockSpec((d, nc), lambda i: (0, 0)),
        ],
        out_specs=(pl.BlockSpec((tm, d), lambda i: (i, 0)), pl.BlockSpec((tm, nc), lambda i: (i, 0))),
        compiler_params=_cparams(("parallel",)),
        name="rmsnorm_router",
    )(x, g.reshape(1, d), w_hi, w_lo)


def _mm_kernel(a_ref, b_ref, o_ref):
    o_ref[...] = _dot(a_ref[...], b_ref[...]).astype(o_ref.dtype)


def matmul(a, b, out_dtype, tm, tn):
    m, k = a.shape
    n = b.shape[1]
    return pl.pallas_call(
        _mm_kernel,
        out_shape=jax.ShapeDtypeStruct((m, n), out_dtype),
        grid=(m // tm, n // tn),
        in_specs=[pl.BlockSpec((tm, k), lambda i, j: (i, 0)), pl.BlockSpec((k, tn), lambda i, j: (0, j))],
        out_specs=pl.BlockSpec((tm, tn), lambda i, j: (i, j)),
        compiler_params=_cparams(("parallel", "parallel")),
        name="matmul",
    )(a, b)


def _mm_residual_kernel(a_ref, b_ref, r_ref, o_ref, *, tm, segments):
    acc = _dot(a_ref[...], b_ref[...])
    rows = pl.program_id(0) * tm + lax.broadcasted_iota(jnp.int32, (tm, 1), 0)
    o_ref[...] = jnp.where(_seq_valid(rows, segments), r_ref[...] + acc, 0.0)


def matmul_residual(a, b, res, segments, tm, tn):
    m, k = a.shape
    n = b.shape[1]
    return pl.pallas_call(
        functools.partial(_mm_residual_kernel, tm=tm, segments=segments),
        out_shape=jax.ShapeDtypeStruct((m, n), F32),
        grid=(m // tm, n // tn),
        in_specs=[
            pl.BlockSpec((tm, k), lambda i, j: (i, 0)),
            pl.BlockSpec((k, tn), lambda i, j: (0, j)),
            pl.BlockSpec((tm, tn), lambda i, j: (i, j)),
        ],
        out_specs=pl.BlockSpec((tm, tn), lambda i, j: (i, j)),
        compiler_params=_cparams(("parallel", "parallel")),
        name="matmul_residual",
    )(a, b, res)


def _latent_kernel(h_ref, w_ref, gq_ref, gkv_ref, cq_ref, ckv_ref, kpe_ref, *, q_lora, kv_lora):
    p = _dot(h_ref[...], w_ref[...])
    q = p[:, :q_lora]
    kv = p[:, q_lora : q_lora + kv_lora]
    cq_ref[...] = (q * lax.rsqrt(jnp.mean(q * q, axis=-1, keepdims=True) + EPS) * gq_ref[...]).astype(cq_ref.dtype)
    ckv_ref[...] = (kv * lax.rsqrt(jnp.mean(kv * kv, axis=-1, keepdims=True) + EPS) * gkv_ref[...]).astype(
        ckv_ref.dtype
    )
    kpe_ref[...] = p[:, q_lora + kv_lora :]


def latent_proj(h, w_lat, gq, gkv, dm, tm=ROW_TILE):
    r, d = h.shape
    nl = w_lat.shape[1]
    return pl.pallas_call(
        functools.partial(_latent_kernel, q_lora=dm.q_lora, kv_lora=dm.kv_lora),
        out_shape=(
            jax.ShapeDtypeStruct((r, dm.q_lora), BF16),
            jax.ShapeDtypeStruct((r, dm.kv_lora), BF16),
            jax.ShapeDtypeStruct((r, LANE), F32),
        ),
        grid=(r // tm,),
        in_specs=[
            pl.BlockSpec((tm, d), lambda i: (i, 0)),
            pl.BlockSpec((d, nl), lambda i: (0, 0)),
            pl.BlockSpec((1, dm.q_lora), lambda i: (0, 0)),
            pl.BlockSpec((1, dm.kv_lora), lambda i: (0, 0)),
        ],
        out_specs=(
            pl.BlockSpec((tm, dm.q_lora), lambda i: (i, 0)),
            pl.BlockSpec((tm, dm.kv_lora), lambda i: (i, 0)),
            pl.BlockSpec((tm, LANE), lambda i: (i, 0)),
        ),
        compiler_params=_cparams(("parallel",)),
        name="latent_proj",
    )(h, w_lat, gq.reshape(1, -1), gkv.reshape(1, -1))


def _rope_tail(c, cos_t, sin_a, sin_b, half):
    return c * cos_t + pltpu.roll(c, LANE - half, 1) * sin_a + pltpu.roll(c, half, 1) * sin_b


def _qkv_kernel(cq_ref, ckv_ref, kpe_ref, cos_ref, sa_ref, sb_ref, wq_ref, wk_ref, wv_ref, gq_ref, gk_ref,
                q_ref, k_ref, v_ref, *, dm):
    hp, nope, qk = dm.head_pad, dm.nope, dm.qk_dim
    half = dm.rope // 2
    scale = qk ** -0.5
    ckv = ckv_ref[...]
    q = _dot(cq_ref[...], wq_ref[...])
    kn = _dot(ckv, wk_ref[...])
    v_ref[...] = _dot(ckv, wv_ref[...]).astype(v_ref.dtype)
    pe = kpe_ref[...]
    pe_ss = jnp.sum(pe * pe, axis=-1, keepdims=True)
    cos_t, sin_a, sin_b = cos_ref[...], sa_ref[...], sb_ref[...]
    gq = gq_ref[...]
    gk = gk_ref[...]
    for h in range(dm.n_heads):
        qh = q[:, h * hp : (h + 1) * hp]
        r = lax.rsqrt(jnp.sum(qh * qh, axis=-1, keepdims=True) * (1.0 / qk) + EPS) * scale
        qa = qh[:, :nope] * r * gq[:, :nope]
        qc = _rope_tail(qh[:, nope:] * r * gq[:, nope:], cos_t, sin_a, sin_b, half)
        q_ref[:, h * hp : h * hp + nope] = qa.astype(q_ref.dtype)
        q_ref[:, h * hp + nope : (h + 1) * hp] = qc.astype(q_ref.dtype)
        kh = kn[:, h * nope : (h + 1) * nope]
        rk = lax.rsqrt((jnp.sum(kh * kh, axis=-1, keepdims=True) + pe_ss) * (1.0 / qk) + EPS)
        ka = kh * rk * gk[:, :nope]
        kc = _rope_tail(pe * rk * gk[:, nope:], cos_t, sin_a, sin_b, half)
        k_ref[:, h * hp : h * hp + nope] = ka.astype(k_ref.dtype)
        k_ref[:, h * hp + nope : (h + 1) * hp] = kc.astype(k_ref.dtype)


def qkv_heads(cq, ckv, kpe, rope_tabs, wq, wk, wv, gq, gk, dm, tk, tm=ROW_TILE):
    off = tk.row0 // tm
    hp, nh = dm.head_pad, dm.n_heads
    row = lambda i: (off + i, 0)
    loc = lambda i: (i, 0)
    fixed = lambda i: (0, 0)
    return pl.pallas_call(
        functools.partial(_qkv_kernel, dm=dm),
        out_shape=(
            jax.ShapeDtypeStruct((tk.rows, nh * hp), BF16),
            jax.ShapeDtypeStruct((tk.rows, nh * hp), BF16),
            jax.ShapeDtypeStruct((tk.rows, nh * dm.v_dim), BF16),
        ),
        grid=(tk.rows // tm,),
        in_specs=[
            pl.BlockSpec((tm, dm.q_lora), row),
            pl.BlockSpec((tm, dm.kv_lora), row),
            pl.BlockSpec((tm, LANE), row),
            pl.BlockSpec((tm, LANE), loc),
            pl.BlockSpec((tm, LANE), loc),
            pl.BlockSpec((tm, LANE), loc),
            pl.BlockSpec(wq.shape, fixed),
            pl.BlockSpec(wk.shape, fixed),
            pl.BlockSpec(wv.shape, fixed),
            pl.BlockSpec((1, hp), fixed),
            pl.BlockSpec((1, hp), fixed),
        ],
        out_specs=(
            pl.BlockSpec((tm, nh * hp), loc),
            pl.BlockSpec((tm, nh * hp), loc),
            pl.BlockSpec((tm, nh * dm.v_dim), loc),
        ),
        compiler_params=_cparams(("parallel",)),
        name="qkv_heads",
    )(cq, ckv, kpe, *rope_tabs, wq, wk, wv, gq, gk)


def _attn_kernel(q_ref, k_ref, v_ref, o_ref, *, t, tk, tk_last, unroll):
    q = q_ref[...]
    tq = q.shape[0]
    n_full, rem = divmod(t, tk)

    def chunk(start, size, carry, valid):
        m, l, acc = carry
        k = k_ref[pl.ds(start, size), :]
        v = v_ref[pl.ds(start, size), :]
        s = lax.dot_general(q, k, (((1,), (1,)), ((), ())), preferred_element_type=F32)
        if valid < size:
            col = lax.broadcasted_iota(jnp.int32, s.shape, 1)
            s = jnp.where(col < valid, s, NEG_BIG)
        m_new = jnp.maximum(m, jnp.max(s, axis=-1, keepdims=True))
        alpha = jnp.exp(m - m_new)
        p = jnp.exp(s - m_new)
        l = alpha * l + jnp.sum(p, axis=-1, keepdims=True)
        acc = alpha * acc + _dot(p.astype(v.dtype), v)
        return m_new, l, acc

    carry = (jnp.full((tq, 1), NEG_BIG, F32), jnp.zeros((tq, 1), F32), jnp.zeros((tq, v_ref.shape[1]), F32))
    carry = lax.fori_loop(
        0, n_full, lambda c, cr: chunk(pl.multiple_of(c * tk, tk), tk, cr, tk), carry, unroll=unroll)
    if rem:
        carry = chunk(n_full * tk, tk_last, carry, rem)
    _, l, acc = carry
    o_ref[...] = (acc / l).astype(o_ref.dtype)


def _attn_tiles(tk, max_rows, tkc):
    nq = next(n for n in range(1, tk.tp + 1)
              if tk.tp % n == 0 and (tk.tp // n) % (2 * SUBLANE) == 0 and tk.tp // n <= max_rows)
    n_full, rem = divmod(tk.t, tkc)
    tk_last = min(tkc, _round_up(rem, ROW_TILE)) if rem else 0
    assert n_full * tkc + tk_last <= tk.tp
    return nq, tk_last


def attention(q, k, v, dm, tk, max_rows=1152, tkc=512, unroll=2):
    nq, tk_last = _attn_tiles(tk, max_rows, tkc)
    tq = tk.tp // nq
    hp = dm.head_pad
    return pl.pallas_call(
        functools.partial(_attn_kernel, t=tk.t, tk=tkc, tk_last=tk_last, unroll=unroll),
        out_shape=jax.ShapeDtypeStruct((tk.rows, dm.n_heads * dm.v_dim), F32),
        grid=(tk.batch, dm.n_heads, nq),
        in_specs=[
            pl.BlockSpec((tq, hp), lambda b, h, i: (b * nq + i, h)),
            pl.BlockSpec((tk.tp, hp), lambda b, h, i: (b, h)),
            pl.BlockSpec((tk.tp, dm.v_dim), lambda b, h, i: (b, h)),
        ],
        out_specs=pl.BlockSpec((tq, dm.v_dim), lambda b, h, i: (b * nq + i, h)),
        compiler_params=_cparams(("parallel", "parallel", "arbitrary")),
        name="attention",
    )(q, k, v)


def _group_norm_kernel(a_ref, hy_ref, g_ref, o_ref, *, group):
    wa = a_ref.shape[1]
    g = g_ref[...]
    for src, base in ((a_ref, 0), (hy_ref, wa)):
        for c in range(src.shape[1] // group):
            xc = src[:, c * group : (c + 1) * group]
            y = xc * lax.rsqrt(jnp.mean(xc * xc, axis=-1, keepdims=True) + EPS)
            lo = base + c * group
            o_ref[:, lo : lo + group] = (y * g[:, lo : lo + group]).astype(o_ref.dtype)


def group_norm(a, hy, g, group, tm=ROW_TILE):
    r, wa = a.shape
    wh = hy.shape[1]
    return pl.pallas_call(
        functools.partial(_group_norm_kernel, group=group),
        out_shape=jax.ShapeDtypeStruct((r, wa + wh), BF16),
        grid=(r // tm,),
        in_specs=[
            pl.BlockSpec((tm, wa), lambda i: (i, 0)),
            pl.BlockSpec((tm, wh), lambda i: (i, 0)),
            pl.BlockSpec((1, wa + wh), lambda i: (0, 0)),
        ],
        out_specs=pl.BlockSpec((tm, wa + wh), lambda i: (i, 0)),
        compiler_params=_cparams(("parallel",)),
        name="group_norm",
    )(a, hy, g.reshape(1, -1))


def _conv3_kernel(u_ref, prev_ref, next_ref, w_ref, b_ref, o_ref, *, tm, t, tp, starts_at_row0):
    i = pl.program_id(0)
    u = u_ref[...]
    r = lax.broadcasted_iota(jnp.int32, (tm, 1), 0)
    prev_row = prev_ref[SUBLANE - 1 : SUBLANE, :]
    if starts_at_row0:
        prev_row = jnp.where(i == 0, 0.0, prev_row)
    up = jnp.where(r == 0, prev_row, pltpu.roll(u, 1, 0))
    dn = jnp.where(r == tm - 1, next_ref[0:1, :], pltpu.roll(u, tm - 1, 0))
    w = w_ref[...]
    uc = up * w[0:1, :] + u * w[1:2, :] + dn * w[2:3, :] + b_ref[...]
    pos = (i % (tp // tm)) * tm + r
    o_ref[...] = jnp.where(pos < t, uc, 0.0)


def conv3(u, w, b, tk, n_split, tm=ROW_TILE):
    cols = u.shape[1]
    tc = cols // n_split
    off = tk.row0 // tm
    sub = tm // SUBLANE
    last = u.shape[0] // SUBLANE - 1
    return pl.pallas_call(
        functools.partial(_conv3_kernel, tm=tm, t=tk.t, tp=tk.tp, starts_at_row0=(tk.row0 == 0)),
        out_shape=jax.ShapeDtypeStruct((n_split, tk.rows, tc), F32),
        grid=(tk.rows // tm, n_split),
        in_specs=[
            pl.BlockSpec((tm, tc), lambda i, j: (off + i, j)),
            pl.BlockSpec((SUBLANE, tc), lambda i, j: (jnp.maximum((off + i) * sub - 1, 0), j)),
            pl.BlockSpec((SUBLANE, tc), lambda i, j: (jnp.minimum((off + i + 1) * sub, last), j)),
            pl.BlockSpec((3, tc), lambda i, j: (0, j)),
            pl.BlockSpec((1, tc), lambda i, j: (0, j)),
        ],
        out_specs=pl.BlockSpec((None, tm, tc), lambda i, j: (j, i, 0)),
        compiler_params=_cparams(("parallel", "parallel")),
        name="conv3",
    )(u, u, u, w, b.reshape(1, -1))


def _taps_kernel(fvec_ref, w1h_ref, w1l_ref, b1_ref, f1_ref, w2h_ref, w2l_ref, b2_ref, f2_ref, w3h_ref, w3l_ref,
                 dl_ref, o_ref, *, tr, t, n, n_orders, bands):
    p = pl.program_id(0) * tr + lax.broadcasted_iota(jnp.int32, (tr, 1), 0)
    fwd = p < t
    bwd = p > n - t
    lag = jnp.where(fwd | bwd, jnp.where(fwd, p, n - p), 0).astype(F32)
    t_norm = lag * (1.0 / (t - 1))
    arg = ((2.0 * math.pi) * lag / t) * fvec_ref[...]
    lane = lax.broadcasted_iota(jnp.int32, (1, LANE), 1)
    cos_l = (lane >= 1) & (lane <= bands)
    sin_l = (lane > bands) & (lane <= 2 * bands)
    feats = jnp.where(lane == 0, t_norm, jnp.where(cos_l, jnp.cos(arg), jnp.where(sin_l, -jnp.sin(arg), 0.0)))
    hid = jnp.sin(f1_ref[...] * (_dot3(feats, w1h_ref[...], w1l_ref[...]) + b1_ref[...]))
    hid = jnp.sin(f2_ref[...] * (_dot3(hid, w2h_ref[...], w2l_ref[...]) + b2_ref[...]))
    decay = jnp.exp(-t_norm * dl_ref[...])
    for o in range(n_orders):
        f_fwd = _dot3(hid, w3h_ref[2 * o], w3l_ref[2 * o])
        f_bwd = _dot3(hid, w3h_ref[2 * o + 1], w3l_ref[2 * o + 1])
        o_ref[o] = jnp.where(fwd, f_fwd, jnp.where(bwd, f_bwd, 0.0)) * decay


def hyena_taps(fp, tk, dm, tr=ROW_TILE, tc=2048):
    n = tk.n
    c = dm.hy_width
    tc = min(tc, c)
    no = dm.hy_order
    fixed2 = lambda i, j: (0, 0)
    sq = pl.BlockSpec((LANE, LANE), fixed2)
    vec = pl.BlockSpec((1, LANE), fixed2)
    return pl.pallas_call(
        functools.partial(_taps_kernel, tr=tr, t=tk.t, n=n, n_orders=no, bands=(dm.filt_emb - 1) // 2),
        out_shape=jax.ShapeDtypeStruct((no, n, c), F32),
        grid=(n // tr, c // tc),
        in_specs=[vec, sq, sq, vec, vec, sq, sq, vec, vec,
                  pl.BlockSpec((2 * no, LANE, tc), lambda i, j: (0, 0, j)),
                  pl.BlockSpec((2 * no, LANE, tc), lambda i, j: (0, 0, j)),
                  pl.BlockSpec((1, tc), lambda i, j: (0, j))],
        out_specs=pl.BlockSpec((no, tr, tc), lambda i, j: (0, i, j)),
        compiler_params=_cparams(("parallel", "parallel")),
        name="hyena_taps",
    )(fp["fvec"], fp["w1h"], fp["w1l"], fp["b1"], fp["f1"], fp["w2h"], fp["w2l"], fp["b2"], fp["f2"], fp["w3h"],
      fp["w3l"], fp["deltas"])


def _pdft_fwd_kernel(g_ref, z_ref, ar_ref, ai_ref, xs_ref, *, parts, rows, n1):
    xs_ref[...] = jnp.zeros_like(xs_ref)
    half = xs_ref.shape[0] // parts
    for r in range(SUBLANE):
        for p in range(parts):
            xs_ref[p * half : p * half + rows, :] = z_ref[p, :, r, :]
        res = _dot(g_ref[r], xs_ref[...].astype(BF16))
        ar_ref[:, r, :] = res[:n1]
        ai_ref[:, r, :] = res[n1:]


def pdft_fwd(g, z, n1, which, dm):
    s0, sn = which
    _, parts, rows, _, c = z.shape
    cb = min(dm.dft_cols, c)
    kpad = g.shape[2]
    out = jax.ShapeDtypeStruct((sn, n1, TIME_BLOCK, c), F32)
    oblk = pl.BlockSpec((None, n1, SUBLANE, cb), lambda s, j, k: (s, 0, j, k))
    return pl.pallas_call(
        functools.partial(_pdft_fwd_kernel, parts=parts, rows=rows, n1=n1),
        out_shape=(out, out),
        grid=(sn, TIME_BLOCK // SUBLANE, c // cb),
        in_specs=[
            pl.BlockSpec((SUBLANE, 2 * n1, kpad), lambda s, j, k: (j, 0, 0)),
            pl.BlockSpec((None, parts, rows, SUBLANE, cb), lambda s, j, k: (s0 + s, 0, 0, j, k)),
        ],
        out_specs=(oblk, oblk),
        scratch_shapes=[pltpu.VMEM((kpad, cb), F32)],
        compiler_params=_cparams(("parallel", "parallel", "parallel")),
        name="pdft_fwd",
    )(g, z)


def _spectrum_kernel(ar_ref, ai_ref, f_ref, h_ref):
    f = f_ref[...]
    for k in range(ar_ref.shape[0]):
        a = jnp.concatenate([ar_ref[k], ai_ref[k]], axis=0).astype(BF16)
        h_ref[k] = _dot(f, a)


def filter_spectrum(ar, ai, f2f, dm, kb=SUBLANE):
    no, n1, _, c = ar.shape
    cb = min(dm.dft_cols, c)
    blk = pl.BlockSpec((None, kb, TIME_BLOCK, cb), lambda o, i, j: (o, i, 0, j))
    return pl.pallas_call(
        _spectrum_kernel,
        out_shape=jax.ShapeDtypeStruct((no, n1, 2 * TIME_BLOCK, c), F32),
        grid=(no, n1 // kb, c // cb),
        in_specs=[blk, blk, pl.BlockSpec(f2f.shape, lambda o, i, j: (0, 0))],
        out_specs=pl.BlockSpec((None, kb, 2 * TIME_BLOCK, cb), lambda o, i, j: (o, i, 0, j)),
        compiler_params=_cparams(("parallel", "parallel", "parallel")),
        name="filter_spectrum",
    )(ar, ai, f2f)


def _freq_mul_kernel(ar_ref, ai_ref, h_ref, ff_ref, fi_ref, br_ref, bi_ref):
    ff = ff_ref[...]
    fi = fi_ref[...]
    nb = TIME_BLOCK
    for k in range(ar_ref.shape[0]):
        x = _dot(ff, jnp.concatenate([ar_ref[k], ai_ref[k]], axis=0).astype(BF16))
        h = h_ref[k]
        xr, xi = x[:nb], x[nb:]
        hr, hi = h[:nb], h[nb:]
        y = jnp.concatenate([xr * hr - xi * hi, xr * hi + xi * hr], axis=0).astype(BF16)
        b = _dot(fi, y)
        br_ref[k] = b[:nb]
        bi_ref[k] = b[nb:]


def freq_multiply(ar, ai, h, order, f2f, f2i, dm, kb=SUBLANE):
    _, n1, _, c = ar.shape
    cb = min(dm.dft_cols, c)
    ablk = pl.BlockSpec((None, kb, TIME_BLOCK, cb), lambda i, j: (0, i, 0, j))
    blk = pl.BlockSpec((kb, TIME_BLOCK, cb), lambda i, j: (i, 0, j))
    out = jax.ShapeDtypeStruct((n1, TIME_BLOCK, c), F32)
    return pl.pallas_call(
        _freq_mul_kernel,
        out_shape=(out, out),
        grid=(n1 // kb, c // cb),
        in_specs=[ablk, ablk,
                  pl.BlockSpec((None, kb, 2 * TIME_BLOCK, cb), lambda i, j: (order, i, 0, j)),
                  pl.BlockSpec(f2f.shape, lambda i, j: (0, 0)),
                  pl.BlockSpec(f2i.shape, lambda i, j: (0, 0))],
        out_specs=(blk, blk),
        compiler_params=_cparams(("parallel", "parallel")),
        name="freq_multiply",
    )(ar, ai, h, f2f, f2i)


def _pdft_inv_gate_kernel(g_ref, br_ref, bi_ref, gate_ref, z_ref, d_ref, o_ref, bs_ref, *, n1, rows):
    bs_ref[...] = jnp.zeros_like(bs_ref)
    half = bs_ref.shape[0] // 2
    mh = g_ref.shape[1] // 2
    d = d_ref[...]
    for r in range(SUBLANE):
        bs_ref[0:n1, :] = br_ref[:, r, :]
        bs_ref[half : half + n1, :] = bi_ref[:, r, :]
        y = _dot(g_ref[r], bs_ref[...].astype(BF16))
        for b in range(2):
            o_ref[b, :, r, :] = gate_ref[b, :, r, :] * (y[b * mh : b * mh + rows] + d * z_ref[b, :, r, :])


def pdft_inv_gate(ginv, br, bi, sig, gate_idx, z, z_idx, d, tk, dm):
    n1, _, c = br.shape
    cb = min(dm.dft_cols, c)
    kh = tk.kh
    kpad = ginv.shape[2]
    seq = lambda idx: pl.BlockSpec((None, 2, kh, SUBLANE, cb), lambda j, k: (idx, 0, 0, j, k))
    bblk = pl.BlockSpec((n1, SUBLANE, cb), lambda j, k: (0, j, k))
    return pl.pallas_call(
        functools.partial(_pdft_inv_gate_kernel, n1=n1, rows=kh),
        out_shape=jax.ShapeDtypeStruct((1, 2, kh, TIME_BLOCK, c), F32),
        grid=(TIME_BLOCK // SUBLANE, c // cb),
        in_specs=[
            pl.BlockSpec((SUBLANE, ginv.shape[1], kpad), lambda j, k: (j, 0, 0)),
            bblk,
            bblk,
            seq(gate_idx),
            seq(z_idx),
            pl.BlockSpec((1, cb), lambda j, k: (0, k)),
        ],
        out_specs=seq(0),
        scratch_shapes=[pltpu.VMEM((kpad, cb), F32)],
        compiler_params=_cparams(("parallel", "parallel")),
        name="pdft_inv_gate",
    )(ginv, br, bi, sig, z, d.reshape(1, c))


def _dft_real_form(phase_num, n, sign, scale=1.0):
    ang = (sign * 2.0 * math.pi / n) * (phase_num % n).astype(F32)
    c = jnp.cos(ang) * scale
    s = jnp.sin(ang) * scale
    top = jnp.concatenate([c, -s], axis=-1)
    bot = jnp.concatenate([s, c], axis=-1)
    return jnp.concatenate([top, bot], axis=-2).astype(BF16)


def dft_tables(tk):
    n1, n, nb = tk.n1, tk.n, TIME_BLOCK
    i32 = jnp.int32
    n2 = jnp.arange(nb, dtype=i32)[:, None, None]
    k1 = jnp.arange(n1, dtype=i32)[None, :, None]

    def major(cols):
        col = jnp.arange(cols, dtype=i32)[None, None, :]
        return k1 * (nb * col + n2)

    sig_half = _round_up(tk.kh, 64)
    tap_half = _round_up(n1, LANE)
    out_half = _round_up(tk.kh, SUBLANE)
    g_sig = _dft_real_form(major(sig_half), n, -1.0)
    g_tap = _dft_real_form(major(tap_half), n, -1.0)[:, :, :tap_half]
    rows = jnp.arange(out_half, dtype=i32)[None, :, None]
    kcol = jnp.arange(tap_half, dtype=i32)[None, None, :]
    g_inv = _dft_real_form(kcol * (nb * rows + n2), n, 1.0, 1.0 / n)
    kk = jnp.arange(nb, dtype=i32)
    minor = kk[:, None] * kk[None, :]
    f2f = _dft_real_form(minor, nb, -1.0)
    f2i = _dft_real_form(minor, nb, 1.0)
    return dict(g_sig=g_sig, g_tap=g_tap, g_inv=g_inv, f2f=f2f, f2i=f2i)


def filter_params(p, l, dm):
    emb, hid, c, no = dm.filt_emb, dm.filt_hidden, dm.hy_width, dm.hy_order
    bands = (emb - 1) // 2

    def pad2(w, r, cc):
        return jnp.pad(w.astype(F32), ((0, r - w.shape[0]), (0, cc - w.shape[1])))

    def padv(v):
        return jnp.pad(v.astype(F32), (0, LANE - v.shape[0])).reshape(1, LANE)

    freqs = jnp.linspace(1e-4, bands - 1, bands, dtype=F32)
    fvec = jnp.concatenate([jnp.zeros((1,), F32), freqs, freqs, jnp.zeros((LANE - emb,), F32)]).reshape(1, LANE)
    w1h, w1l = _split_bf16(pad2(p["filt_w1"][l], LANE, LANE))
    w2h, w2l = _split_bf16(pad2(p["filt_w2"][l], LANE, LANE))
    w3 = p["filt_w3"][l].astype(F32).reshape(hid, 2 * no, c).transpose(1, 0, 2)
    w3h, w3l = _split_bf16(jnp.pad(w3, ((0, 0), (0, LANE - hid), (0, 0))))
    deltas = jnp.abs(jnp.linspace(math.log(dm.decay_target) / dm.slow_decay,
                                  math.log(dm.decay_target) / dm.fast_decay, c, dtype=F32)).reshape(1, c)
    return dict(fvec=fvec, w1h=w1h, w1l=w1l, b1=padv(p["filt_b1"][l]), f1=padv(p["filt_freq1"][l]),
                w2h=w2h, w2l=w2l, b2=padv(p["filt_b2"][l]), f2=padv(p["filt_freq2"][l]),
                w3h=w3h, w3l=w3l, deltas=deltas)


def hyena(u, p, l, tk, tabs, dm):
    c, no = dm.hy_width, dm.hy_order
    n1, kh = tk.n1, tk.kh
    assert tk.batch == 2, "the two sequences of a trunk are the real and imaginary parts of one transform"
    sig = conv3(u, p["conv_w"][l], p["conv_b"][l], tk, no + 1)
    sig = sig.reshape(no + 1, tk.batch, kh, TIME_BLOCK, c)
    taps = hyena_taps(filter_params(p, l, dm), tk, dm)
    tr, ti = pdft_fwd(tabs["g_tap"], taps.reshape(no, 1, n1, TIME_BLOCK, c), n1, (0, no), dm)
    spec = filter_spectrum(tr, ti, tabs["f2f"], dm)
    z = sig
    for o in range(no):
        ar, ai = pdft_fwd(tabs["g_sig"], z, n1, (0, 1), dm)
        br, bi = freq_multiply(ar, ai, spec, o, tabs["f2f"], tabs["f2i"], dm)
        z = pdft_inv_gate(tabs["g_inv"], br, bi, sig, o + 1, z, 0, p["hyena_d"][l, o], tk, dm)
    return z.reshape(tk.rows, c)


def _row_copy(src_ref, src_row, dst_ref, dst_row, sem):
    return pltpu.make_async_copy(src_ref.at[pl.ds(src_row, 1)], dst_ref.at[pl.ds(dst_row, 1)], sem)


DMA_ISSUE_UNROLL = 8
DMA_PRIORITIES = 2


def _gather_rows_kernel(nrows_ref, idx_ref, src_ref, o_ref, buf_ref, sem, *, tg):
    live = pl.program_id(0) * tg < nrows_ref[0]

    @pl.when(live)
    def _():
        def issue(i, _):
            for q in range(DMA_PRIORITIES):
                row = i * DMA_PRIORITIES + q
                _row_copy(src_ref, idx_ref[row], buf_ref, row, sem).start(priority=q)
            return 0

        lax.fori_loop(0, tg // DMA_PRIORITIES, issue, 0, unroll=DMA_ISSUE_UNROLL // DMA_PRIORITIES)

        def drain(i, _):
            _row_copy(src_ref, 0, buf_ref, i, sem).wait()
            return 0

        lax.fori_loop(0, tg, drain, 0, unroll=DMA_ISSUE_UNROLL)
        o_ref[...] = buf_ref[...].astype(o_ref.dtype)

    @pl.when(jnp.logical_not(live))
    def _():
        o_ref[...] = jnp.zeros_like(o_ref)


def gather_rows(src, idx, n_live, out_dtype, tg=ROW_TILE):
    n = idx.shape[0]
    d = src.shape[1]
    return pl.pallas_call(
        functools.partial(_gather_rows_kernel, tg=tg),
        out_shape=jax.ShapeDtypeStruct((n, d), out_dtype),
        grid_spec=pltpu.PrefetchScalarGridSpec(
            num_scalar_prefetch=1,
            grid=(n // tg,),
            in_specs=[
                pl.BlockSpec((tg,), lambda i, nr: (i,), memory_space=pltpu.SMEM),
                pl.BlockSpec(memory_space=pl.ANY),
            ],
            out_specs=pl.BlockSpec((tg, d), lambda i, nr: (i, 0)),
            scratch_shapes=[pltpu.VMEM((tg, d), src.dtype), pltpu.SemaphoreType.DMA],
        ),
        compiler_params=_cparams(("arbitrary",)),
        name="moe_gather",
    )(n_live, idx, src)


def _expert_up_kernel(be_ref, first_ref, nb_ref, x_ref, wg_ref, wu_ref, o_ref, wgs_ref, wus_ref):
    m = pl.program_id(1)

    @pl.when(m < nb_ref[0])
    def _():
        @pl.when(first_ref[m] == 1)
        def _():
            wgs_ref[...] = wg_ref[...].astype(BF16)
            wus_ref[...] = wu_ref[...].astype(BF16)

        x = x_ref[...]
        g = _dot(x, wgs_ref[...])
        u = _dot(x, wus_ref[...])
        o_ref[...] = (g * jax.nn.sigmoid(g) * u).astype(o_ref.dtype)

    @pl.when(m >= nb_ref[0])
    def _():
        o_ref[...] = jnp.zeros_like(o_ref)


def expert_up(xs, w_gate, w_up, l, blk_expert, blk_first, n_blocks, dm):
    cap, d = xs.shape
    tm, tf = dm.moe_rows, min(dm.moe_up_cols, dm.d_expert)
    nblk = cap // tm
    wspec = pl.BlockSpec((None, None, d, tf), lambda j, m, be, fi, nb: (l, be[m], 0, j))
    return pl.pallas_call(
        _expert_up_kernel,
        out_shape=jax.ShapeDtypeStruct((cap, dm.d_expert), BF16),
        grid_spec=pltpu.PrefetchScalarGridSpec(
            num_scalar_prefetch=3,
            grid=(dm.d_expert // tf, nblk),
            in_specs=[pl.BlockSpec((tm, d), lambda j, m, be, fi, nb: (m, 0)), wspec, wspec],
            out_specs=pl.BlockSpec((tm, tf), lambda j, m, be, fi, nb: (m, j)),
            scratch_shapes=[pltpu.VMEM((d, tf), BF16), pltpu.VMEM((d, tf), BF16)],
        ),
        compiler_params=_cparams(("arbitrary", "arbitrary")),
        name="moe_expert_up",
    )(blk_expert, blk_first, n_blocks, xs, w_gate, w_up)


def _expert_down_kernel(be_ref, first_ref, nb_ref, h_ref, wd_ref, o_ref, wds_ref):
    m = pl.program_id(1)

    @pl.when(m < nb_ref[0])
    def _():
        @pl.when(first_ref[m] == 1)
        def _():
            wds_ref[...] = wd_ref[...].astype(BF16)

        o_ref[...] = _dot(h_ref[...], wds_ref[...]).astype(o_ref.dtype)

    @pl.when(m >= nb_ref[0])
    def _():
        o_ref[...] = jnp.zeros_like(o_ref)


def expert_down(hs, w_down, l, blk_expert, blk_first, n_blocks, dm):
    cap, f = hs.shape
    d = w_down.shape[3]
    tm, tn = dm.moe_rows, min(dm.moe_down_cols, d)
    nblk = cap // tm
    return pl.pallas_call(
        _expert_down_kernel,
        out_shape=jax.ShapeDtypeStruct((cap, d), F32),
        grid_spec=pltpu.PrefetchScalarGridSpec(
            num_scalar_prefetch=3,
            grid=(d // tn, nblk),
            in_specs=[pl.BlockSpec((tm, f), lambda j, m, be, fi, nb: (m, 0)),
                      pl.BlockSpec((None, None, f, tn), lambda j, m, be, fi, nb: (l, be[m], 0, j))],
            out_specs=pl.BlockSpec((tm, tn), lambda j, m, be, fi, nb: (m, j)),
            scratch_shapes=[pltpu.VMEM((f, tn), BF16)],
        ),
        compiler_params=_cparams(("arbitrary", "arbitrary")),
        name="moe_expert_down",
    )(blk_expert, blk_first, n_blocks, hs, w_down)


def _combine_kernel(pos_ref, x_ref, w_ref, ys_ref, o_ref, buf_ref, sem, *, tm, top_k):
    def issue(i, _):
        for k in range(top_k):
            _row_copy(ys_ref, pos_ref[i * top_k + k], buf_ref.at[k], i, sem).start(priority=k % DMA_PRIORITIES)
        return 0

    lax.fori_loop(0, tm, issue, 0, unroll=DMA_ISSUE_UNROLL // top_k)

    def drain(i, _):
        for k in range(top_k):
            _row_copy(ys_ref, 0, buf_ref.at[k], i, sem).wait()
        return 0

    lax.fori_loop(0, tm, drain, 0, unroll=DMA_ISSUE_UNROLL // top_k)
    w = w_ref[...]
    acc = x_ref[...]
    for k in range(top_k):
        acc = acc + w[:, k : k + 1] * buf_ref[k]
    o_ref[...] = acc


def moe_combine(x, ys, pos, wts, dm, tm=ROW_TILE):
    r, d = x.shape
    k = dm.top_k
    return pl.pallas_call(
        functools.partial(_combine_kernel, tm=tm, top_k=k),
        out_shape=jax.ShapeDtypeStruct((r, d), F32),
        grid=(r // tm,),
        in_specs=[
            pl.BlockSpec((tm * k,), lambda i: (i,), memory_space=pltpu.SMEM),
            pl.BlockSpec((tm, d), lambda i: (i, 0)),
            pl.BlockSpec((tm, k), lambda i: (i, 0)),
            pl.BlockSpec(memory_space=pl.ANY),
        ],
        out_specs=pl.BlockSpec((tm, d), lambda i: (i, 0)),
        scratch_shapes=[pltpu.VMEM((k, tm, d), ys.dtype), pltpu.SemaphoreType.DMA],
        compiler_params=_cparams(("arbitrary",)),
        name="moe_combine",
    )(pos.reshape(-1), x, wts, ys)


def route(logits, b_group, b_expert, dm):
    ng, epg = dm.n_groups, dm.epg
    n = logits.shape[0]
    g_prob = jax.nn.softmax(logits[:, :ng] + b_group.astype(F32), axis=-1)
    g_top, g_idx = _top_k(g_prob, 1)
    e_logits = (logits[:, ng : ng + ng * epg] + b_expert.astype(F32)).reshape(n, ng, epg)
    in_group = jnp.take_along_axis(e_logits, g_idx[:, :, None], axis=1)[:, 0]
    e_top, e_idx = _top_k(in_group, dm.top_k)
    weights = g_top * jax.nn.softmax(e_top, axis=-1)
    return (g_idx * epg + e_idx).astype(jnp.int32), weights


def _top_k(x, k):
    cols = lax.broadcasted_iota(jnp.int32, x.shape, 1)
    vals, idxs = [], []
    for _ in range(k):
        i = jnp.argmax(x, axis=-1).astype(jnp.int32)[:, None]
        vals.append(jnp.take_along_axis(x, i, axis=-1))
        idxs.append(i)
        x = jnp.where(cols == i, -jnp.inf, x)
    return jnp.concatenate(vals, axis=-1), jnp.concatenate(idxs, axis=-1)


def _real_rows(a, segments):
    return jnp.concatenate([a[s : s + t] for s, t in segments], axis=0)


def _spread_rows(a, segments, n_rows):
    out, off, end = [], 0, 0
    for s, t in segments:
        out += [jnp.zeros((s - end,) + a.shape[1:], a.dtype), a[off : off + t]]
        off, end = off + t, s + t
    out.append(jnp.zeros((n_rows - end,) + a.shape[1:], a.dtype))
    return jnp.concatenate(out, axis=0)


def dispatch_plan(expert_ids, weights, segments, n_rows, dm):
    ne, k, tm = dm.n_experts, dm.top_k, dm.moe_rows
    i32 = jnp.int32
    vr = jnp.asarray(np.concatenate([np.arange(s, s + t) for s, t in segments]), i32)
    n_tok = vr.shape[0]
    n_slots = n_tok * k
    flat_e = _real_rows(expert_ids, segments).reshape(-1)
    onehot = (flat_e[:, None] == jnp.arange(ne, dtype=i32)[None, :]).astype(i32)
    counts = jnp.sum(onehot, axis=0)
    rank = jnp.take_along_axis(jnp.cumsum(onehot, axis=0), flat_e[:, None], axis=1)[:, 0] - 1
    padded = (counts + tm - 1) // tm * tm
    pend = jnp.cumsum(padded)
    pstart = pend - padded
    dest = pstart[flat_e] + rank
    nblk = -(-n_slots // tm) + ne
    cap = nblk * tm
    slot_row = jnp.repeat(vr, k)
    row_of = jnp.zeros((cap,), i32).at[dest].set(slot_row)
    n_used = (pend[-1] // tm).astype(i32)
    blk_start = jnp.arange(nblk, dtype=i32) * tm
    be = jnp.minimum(jnp.searchsorted(pend, blk_start, side="right"), ne - 1).astype(i32)
    last_used = be[jnp.maximum(n_used - 1, 0)]
    be = jnp.where(jnp.arange(nblk) < n_used, be, last_used)
    first = jnp.concatenate([jnp.ones((1,), i32), (be[1:] != be[:-1]).astype(i32)])
    pos = _spread_rows(dest.reshape(n_tok, k), segments, n_rows)
    wts = _spread_rows(_real_rows(weights.astype(F32), segments), segments, n_rows)
    return dict(row_of=row_of, blk_expert=be, blk_first=first, n_used=n_used.reshape(1), pos=pos, wts=wts)


def moe_ffn(x, h, logits, p, l, segments, dm):
    ids, weights = route(logits, p["b_router_group"][l], p["b_router_expert"][l], dm)
    plan = dispatch_plan(ids, weights, segments, x.shape[0], dm)
    xs = gather_rows(h, plan["row_of"], plan["n_used"] * dm.moe_rows, BF16)
    hs = expert_up(xs, p["w_gate"], p["w_up"], l, plan["blk_expert"], plan["blk_first"], plan["n_used"], dm)
    ys = expert_down(hs, p["w_down"], l, plan["blk_expert"], plan["blk_first"], plan["n_used"], dm)
    return moe_combine(x, ys, plan["pos"], plan["wts"], dm)


def rope_tables(tk, dm):
    half = dm.rope // 2
    pos = jnp.arange(tk.tp, dtype=F32)
    inv = dm.rope_theta ** (-jnp.arange(half, dtype=F32) / half)
    ang = pos[:, None] * inv[None, :]
    cos, sin = jnp.cos(ang), jnp.sin(ang)
    z = jnp.zeros((tk.tp, LANE - 2 * half), F32)
    zh = jnp.zeros((tk.tp, half), F32)
    tabs = (jnp.concatenate([cos, cos, z], axis=1),
            jnp.concatenate([-sin, zh, z], axis=1),
            jnp.concatenate([zh, sin, z], axis=1))
    return tuple(jnp.tile(t, (tk.batch, 1)) for t in tabs)


def layer_weights(p, l, dm):
    o1, o2 = dm.q_lora, dm.q_lora + dm.kv_lora
    o3 = o2 + dm.rope
    w_in = p["w_in"][l]
    w_lat = jnp.pad(w_in[:, :o3], ((0, 0), (0, LANE - dm.rope))).astype(BF16)
    w_hy = w_in[:, o3:].astype(BF16)
    nh, qk, hp = dm.n_heads, dm.qk_dim, dm.head_pad
    wq = jnp.pad(p["w_uq"][l].reshape(dm.q_lora, nh, qk), ((0, 0), (0, 0), (0, hp - qk)))
    wq = wq.reshape(dm.q_lora, nh * hp).astype(BF16)
    gq = jnp.pad(p["q_norm_g"][l].astype(F32), (0, hp - qk)).reshape(1, hp)
    gk = jnp.pad(p["k_norm_g"][l].astype(F32), (0, hp - qk)).reshape(1, hp)
    w_router = jnp.concatenate([p["w_router_group"][l], p["w_router_expert"][l]], axis=1)
    return dict(w_lat=w_lat, w_hy=w_hy, wq=wq, wk=p["w_uk"][l].astype(BF16), wv=p["w_uv"][l].astype(BF16),
                gq=gq, gk=gk, w_out=p["w_out"][l].astype(BF16), w_router=w_router)


def encoder(x_list, p, dm):
    d, nm = dm.d_model, dm.n_meta
    depth = p["norm_mix_g"].shape[0]
    trunks, segments, row0 = [], [], 0
    for x in x_list:
        b, ln, _ = x.shape
        t = ln + nm
        tp = _round_up(t + 1, ROW_TILE)
        trunks.append(Trunk(batch=b, t=t, tp=tp, row0=row0))
        segments += [(row0 + s * tp, t) for s in range(b)]
        row0 += b * tp
    segments = tuple(segments)
    n_rows = row0

    meta = p["meta_tokens"].astype(F32)
    pieces = []
    for xin, tk in zip(x_list, trunks):
        pad = jnp.zeros((tk.tp - tk.t, d), F32)
        for b in range(tk.batch):
            pieces += [meta, xin[b].astype(F32), pad]
    x = jnp.concatenate(pieces, axis=0)

    ropes = [rope_tables(tk, dm) for tk in trunks]
    dfts = [dft_tables(tk) for tk in trunks]
    tm, tn = dm.mm_rows, dm.mm_cols
    for l in range(depth):
        w = layer_weights(p, l, dm)
        h = rmsnorm(x, p["norm_mix_g"][l])
        cq, ckv, kpe = latent_proj(h, w["w_lat"], p["q_latent_g"][l], p["kv_latent_g"][l], dm)
        u = matmul(h, w["w_hy"], F32, tm, min(tn, w["w_hy"].shape[1]))
        mixed = []
        for tk, rt, tabs in zip(trunks, ropes, dfts):
            q, k, v = qkv_heads(cq, ckv, kpe, rt, w["wq"], w["wk"], w["wv"], w["gq"], w["gk"], dm, tk)
            a = attention(q, k, v, dm, tk)
            hy = hyena(u, p, l, tk, tabs, dm)
            mixed.append(group_norm(a, hy, p["out_norm_g"][l], dm.out_group))
        mixed = jnp.concatenate(mixed, axis=0)
        x = matmul_residual(mixed, w["w_out"], x, segments, tm, min(tn, d))
        h, logits = rmsnorm_router(x, p["norm_ffn_g"][l], w["w_router"])
        x = moe_ffn(x, h, logits, p, l, segments, dm)

    outs = []
    for xin, tk in zip(x_list, trunks):
        seqs = [x[tk.row0 + b * tk.tp + nm : tk.row0 + b * tk.tp + tk.t] for b in range(tk.batch)]
        outs.append(jnp.stack(seqs, axis=0).astype(xin.dtype))
    return tuple(outs)


def kernel(x_prompt, x_sample, meta_tokens, norm_mix_g, w_in, q_latent_g, kv_latent_g, w_uq, w_uk, w_uv, q_norm_g, k_norm_g, conv_w, conv_b, filt_w1, filt_b1, filt_freq1, filt_w2, filt_b2, filt_freq2, filt_w3, hyena_d, out_norm_g, w_out, norm_ffn_g, w_router_group, b_router_group, w_router_expert, b_router_expert, w_gate, w_up, w_down):
    p = dict(meta_tokens=meta_tokens, norm_mix_g=norm_mix_g, w_in=w_in, q_latent_g=q_latent_g,
             kv_latent_g=kv_latent_g, w_uq=w_uq, w_uk=w_uk, w_uv=w_uv, q_norm_g=q_norm_g, k_norm_g=k_norm_g,
             conv_w=conv_w, conv_b=conv_b, filt_w1=filt_w1, filt_b1=filt_b1, filt_freq1=filt_freq1,
             filt_w2=filt_w2, filt_b2=filt_b2, filt_freq2=filt_freq2, filt_w3=filt_w3, hyena_d=hyena_d,
             out_norm_g=out_norm_g, w_out=w_out, norm_ffn_g=norm_ffn_g, w_router_group=w_router_group,
             b_router_group=b_router_group, w_router_expert=w_router_expert, b_router_expert=b_router_expert,
             w_gate=w_gate, w_up=w_up, w_down=w_down)
    return encoder((x_prompt, x_sample), p, Dims())
```
